```python
import math
import jax
import jax.numpy as jnp
from jax import lax
import numpy as np

D_MODEL = 4096
BATCH = 4
SEQ = 2048
DEPTH = 4
DEC_BATCH = 8
DEC_SEQ = 1
PAST_LEN = 8192
PAGE_SIZE = 128

MIX_WIDTH = D_MODEL
GROUP_WIDTH = MIX_WIDTH // 4
SSM_CH = 16
G_A = GROUP_WIDTH // SSM_CH
P_A = 64
W_A = G_A * SSM_CH
H_B = 8
DK_B = 128
DV_B = GROUP_WIDTH // H_B
H_C = 4
DH_C = GROUP_WIDTH // (2 * H_C)
N_BUCKETS = 32
MAX_DISTANCE = 128
Q_BLOCK = 128
H_D = 8
DK_D = 128
DV_D = GROUP_WIDTH // H_D
CONV_W = 4
W_D_QKV = H_D * (2 * DK_D + DV_D)
CHUNK = 64
D_FF = 4 * D_MODEL
N_IN = W_A + 2 * H_B * DK_B + 2 * H_B * DV_B + 6 * H_C * DH_C + W_D_QKV + H_D * DV_D + 2 * H_D
EPS = 1e-6

kernel_name = 'hybrid_parallel_heads_s5_hgrn2_diffattn_gdn_step'


def _rms_norm(x, g):
    xf = x.astype(jnp.float32)
    y = xf * lax.rsqrt(jnp.mean(xf * xf, axis=-1, keepdims=True) + EPS)
    return (y * g.astype(jnp.float32)).astype(x.dtype)


def _l2norm(x):
    return x * lax.rsqrt(jnp.sum(x * x, axis=-1, keepdims=True) + EPS)


def _adaln(c, w, b):
    m = jnp.einsum('bd,de->be', jax.nn.silu(c), w) + b
    return jnp.split(m[:, None, :], 6, axis=-1)


def _in_split_points():
    sizes = [W_A,
             H_B * DK_B, H_B * DK_B, H_B * DV_B, H_B * DV_B,
             2 * H_C * DH_C, 2 * H_C * DH_C, 2 * H_C * DH_C,
             W_D_QKV, H_D * DV_D, H_D, H_D]
    pts, acc = [], 0
    for s in sizes[:-1]:
        acc += s
        pts.append(acc)
    return pts


def _complex_affine_combine(e1, e2):
    a1r, a1i, b1r, b1i = e1
    a2r, a2i, b2r, b2i = e2
    return (a2r * a1r - a2i * a1i, a2r * a1i + a2i * a1r,
            a2r * b1r - a2i * b1i + b2r, a2r * b1i + a2i * b1r + b2i)


def _s5(u, lam_re, lam_im, log_dt, b_re, b_im, c_re, c_im, d_skip, w_glu, b_glu, h0_re, h0_im):
    f32 = jnp.float32
    B, T, _ = u.shape
    uf = u.astype(f32).reshape(B, T, G_A, SSM_CH)
    lr, li = lam_re.astype(f32), lam_im.astype(f32)
    dt = jnp.exp(log_dt.astype(f32))[:, None]
    mag = jnp.exp(lr * dt)
    ab_re, ab_im = mag * jnp.cos(li * dt), mag * jnp.sin(li * dt)
    den = lr * lr + li * li
    fr = ((ab_re - 1.0) * lr + ab_im * li) / den
    fi = (ab_im * lr - (ab_re - 1.0) * li) / den
    br, bi = b_re.astype(f32), b_im.astype(f32)
    bb_re = fr[..., None] * br - fi[..., None] * bi
    bb_im = fr[..., None] * bi + fi[..., None] * br
    bu_re = jnp.einsum('btgc,gpc->btgp', uf, bb_re)
    bu_im = jnp.einsum('btgc,gpc->btgp', uf, bb_im)
    a_re = jnp.broadcast_to(ab_re, bu_re.shape)
    a_im = jnp.broadcast_to(ab_im, bu_im.shape)
    acc_re, acc_im, s_re, s_im = lax.associative_scan(
        _complex_affine_combine, (a_re, a_im, bu_re, bu_im), axis=1)
    h0r = h0_re.astype(f32)[:, None]
    h0i = h0_im.astype(f32)[:, None]
    hr = s_re + acc_re * h0r - acc_im * h0i
    hi = s_im + acc_re * h0i + acc_im * h0r
    y = (jnp.einsum('gcp,btgp->btgc', c_re.astype(f32), hr)
         - jnp.einsum('gcp,btgp->btgc', c_im.astype(f32), hi)
         + d_skip.astype(f32) * uf).reshape(B, T, W_A)
    zg = jax.nn.gelu(y)
    out = zg * jax.nn.sigmoid(jnp.einsum('btc,ce->bte', zg, w_glu.astype(f32)) + b_glu.astype(f32))
    return out, hr[:, -1], hi[:, -1]


def _to_chunks(a, L, n):
    B, T = a.shape[:2]
    a = jnp.pad(a, [(0, 0), (0, n * L - T)] + [(0, 0)] * (a.ndim - 2))
    a = a.reshape((B, n, L) + a.shape[2:])
    a = jnp.moveaxis(a, 1, 0)
    return jnp.swapaxes(a, 2, 3)


def _from_chunks(o, T):
    n, B, H, L = o.shape[:4]
    o = jnp.moveaxis(jnp.swapaxes(o, 2, 3), 0, 1)
    return o.reshape((B, n * L) + o.shape[3:])[:, :T]


def _gla_chunked(q, k, v, logf, S0):
    T = q.shape[1]
    L = min(CHUNK, T)
    n = -(-T // L)
    qc, kc, vc, gc = (_to_chunks(a, L, n) for a in (q, k, v, logf))
    tri = jnp.tril(jnp.ones((L, L), dtype=bool))

    def step(S, inp):
        qi, ki, vi, gi = inp
        b = jnp.cumsum(gi, axis=2)
        dec = jnp.exp(jnp.where(tri[:, :, None], b[:, :, :, None, :] - b[:, :, None, :, :], -jnp.inf))
        att = jnp.sum(qi[:, :, :, None, :] * ki[:, :, None, :, :] * dec, axis=-1)
        o = (jnp.einsum('bhtk,bhkv->bhtv', qi * jnp.exp(b), S)
             + jnp.einsum('bhts,bhsv->bhtv', att, vi))
        b_last = b[:, :, -1:, :]
        S_new = (jnp.exp(b_last[:, :, 0, :])[..., None] * S
                 + jnp.einsum('bhlk,bhlv->bhkv', ki * jnp.exp(b_last - b), vi))
        return S_new, o

    S, o = lax.scan(step, S0, (qc, kc, vc, gc))
    return _from_chunks(o, T), S


def _gdn_chunked(q, k, v, beta, logg, S0):
    T = q.shape[1]
    V = v.shape[-1]
    L = min(CHUNK, T)
    n = -(-T // L)
    qc, kc, vc, bc, gc = (_to_chunks(a, L, n) for a in (q, k, v, beta, logg))
    tri = jnp.tril(jnp.ones((L, L), dtype=bool))
    strict = jnp.tril(jnp.ones((L, L), dtype=bool), -1)
    eye = jnp.eye(L, dtype=jnp.float32)

    def step(S, inp):
        qi, ki, vi, bi, gi = inp
        gcum = jnp.cumsum(gi, axis=-1)
        ldec = jnp.exp(jnp.where(tri, gcum[..., :, None] - gcum[..., None, :], -jnp.inf))
        kb = ki * bi[..., None]
        m = jnp.where(strict, jnp.einsum('bhtk,bhsk->bhts', kb, ki) * ldec, 0.0)
        rhs = jnp.concatenate([vi * bi[..., None], kb * jnp.exp(gcum)[..., None]], axis=-1)
        sol = lax.linalg.triangular_solve(eye + m, rhs, left_side=True, lower=True, unit_diagonal=True)
        u, w = sol[..., :V], sol[..., V:]
        v_new = u - jnp.einsum('bhlk,bhkv->bhlv', w, S)
        att = jnp.einsum('bhtk,bhsk->bhts', qi, ki) * ldec
        o = (jnp.einsum('bhlk,bhkv->bhlv', qi * jnp.exp(gcum)[..., None], S)
             + jnp.einsum('bhts,bhsv->bhtv', att, v_new))
        g_last = gcum[..., -1]
        S_new = (jnp.exp(g_last)[..., None, None] * S
                 + jnp.einsum('bhlk,bhlv->bhkv', ki * jnp.exp(g_last[..., None] - gcum)[..., None], v_new))
        return S_new, o

    S, o = lax.scan(step, S0, (qc, kc, vc, bc, gc))
    return _from_chunks(o, T), S


def _causal_conv(x, buf, w):
    T = x.shape[1]
    xx = jnp.concatenate([buf.astype(x.dtype), x], axis=1)
    y = xx[:, 0:T] * w[0]
    for j in range(1, CONV_W):
        y = y + xx[:, j:j + T] * w[j]
    return jax.nn.silu(y), xx[:, T:]


def _t5_bucket(qpos, kpos):
    n = jnp.maximum(qpos[:, None] - kpos[None, :], 0)
    max_exact = N_BUCKETS // 2
    nf = jnp.maximum(n, 1).astype(jnp.float32)
    large = max_exact + (jnp.log(nf / max_exact) / math.log(MAX_DISTANCE / max_exact)
                         * (N_BUCKETS - max_exact)).astype(jnp.int32)
    large = jnp.minimum(large, N_BUCKETS - 1)
    return jnp.where(n < max_exact, n, large)


def _diff_attn_core(q, k, v, qpos, kpos, lam, rel_table):
    s = jnp.einsum('bqhcd,bkhcd->bchqk', q, k) * (DH_C ** -0.5)
    bias = jnp.transpose(rel_table.astype(jnp.float32)[_t5_bucket(qpos, kpos)], (2, 0, 1))
    causal = kpos[None, :] <= qpos[:, None]
    s = jnp.where(causal, s + bias, -jnp.inf)
    p = jax.nn.softmax(s, axis=-1)
    a = p[:, 0] - lam * p[:, 1]
    return jnp.einsum('bhqk,bkhe->bqhe', a, v)


def _diff_attn_prompt(q, k, v, lam, rel_table):
    B, T = q.shape[:2]
    qb = min(Q_BLOCK, T)
    nb = T // qb
    q_blocks = jnp.moveaxis(q.reshape((B, nb, qb) + q.shape[2:]), 1, 0)
    qpos = jnp.arange(T, dtype=jnp.int32).reshape(nb, qb)
    kpos = jnp.arange(T, dtype=jnp.int32)
    out = lax.map(lambda blk: _diff_attn_core(blk[0], k, v, blk[1], kpos, lam, rel_table), (q_blocks, qpos))
    return jnp.moveaxis(out, 0, 1).reshape((B, T) + out.shape[3:])


def _layer(l, x, c, lb, p, past):
    f32 = jnp.float32
    B, T, _ = x.shape
    sh1, sc1, gt1, sh2, sc2, gt2 = _adaln(c, p['w_ada'][l], p['b_ada'][l])
    h = _rms_norm(x, p['g_norm1'][l]) * (1.0 + sc1) + sh1
    z = jnp.einsum('btd,de->bte', h, p['w_in'][l])
    (a_u, b_q, b_f, b_i, b_g, c_q, c_k, c_v, d_qkv, d_g, d_beta, d_a) = jnp.split(z, _in_split_points(), axis=-1)

    if past is None:
        h0_re = jnp.zeros((B, G_A, P_A), f32)
        h0_im = jnp.zeros((B, G_A, P_A), f32)
        s0_b = jnp.zeros((B, H_B, DK_B, DV_B), f32)
        s0_d = jnp.zeros((B, H_D, DK_D, DV_D), f32)
        conv0 = jnp.zeros((B, CONV_W - 1, W_D_QKV), z.dtype)
    else:
        h0_re, h0_im = past['ssm_re'][l], past['ssm_im'][l]
        s0_b, s0_d, conv0 = past['hgrn'][l], past['gdn'][l], past['conv'][l]

    o_a, hr, hi = _s5(a_u, p['ssm_lam_re'][l], p['ssm_lam_im'][l], p['ssm_log_dt'][l],
                      p['ssm_b_re'][l], p['ssm_b_im'][l], p['ssm_c_re'][l], p['ssm_c_im'][l],
                      p['ssm_d'][l], p['ssm_w_glu'][l], p['ssm_b_glu'][l], h0_re, h0_im)

    qb_ = jax.nn.silu(b_q.astype(f32)).reshape(B, T, H_B, DK_B) * (DK_B ** -0.5)
    fl = b_f.astype(f32).reshape(B, T, H_B, DK_B)
    lbh = lb.reshape(H_B, DK_B)
    logf = jnp.logaddexp(jnp.log(lbh), jnp.log1p(-lbh) + jax.nn.log_sigmoid(fl))
    kb_ = (1.0 - lbh) * jax.nn.sigmoid(-fl)
    vb_ = b_i.astype(f32).reshape(B, T, H_B, DV_B)
    o_b, s_b = _gla_chunked(qb_, kb_, vb_, logf, s0_b.astype(f32))
    o_b = (_rms_norm(o_b, p['hgrn_norm'][l]) * jax.nn.silu(b_g.astype(f32).reshape(B, T, H_B, DV_B))).reshape(B, T, H_B * DV_B)

    lam_init = 0.8 - 0.6 * math.exp(-0.3 * l)
    lam = (jnp.exp(jnp.sum(p['diff_lq1'][l].astype(f32) * p['diff_lk1'][l].astype(f32)))
           - jnp.exp(jnp.sum(p['diff_lq2'][l].astype(f32) * p['diff_lk2'][l].astype(f32))) + lam_init)
    qc_ = c_q.astype(f32).reshape(B, T, H_C, 2, DH_C)
    kh = c_k.reshape(B, T, H_C, 2 * DH_C)
    vh = c_v.reshape(B, T, H_C, 2 * DH_C)
    if past is None:
        o_c = _diff_attn_prompt(qc_, kh.astype(f32).reshape(B, T, H_C, 2, DH_C), vh.astype(f32), lam, p['rel_bias'])
    else:
        kp = past['cache_k'][l, past['page_table']].reshape(B, -1, H_C, 2 * DH_C)
        vp = past['cache_v'][l, past['page_table']].reshape(B, -1, H_C, 2 * DH_C)
        past_len = kp.shape[1]
        k_all = jnp.concatenate([kp.astype(f32), kh.astype(f32)], axis=1).reshape(B, past_len + T, H_C, 2, DH_C)
        v_all = jnp.concatenate([vp.astype(f32), vh.astype(f32)], axis=1)
        qpos = past_len + jnp.arange(T, dtype=jnp.int32)
        kpos = jnp.arange(past_len + T, dtype=jnp.int32)
        o_c = _diff_attn_core(qc_, k_all, v_all, qpos, kpos, lam, p['rel_bias'])
    o_c = (_rms_norm(o_c, p['diff_norm'][l]) * (1.0 - lam_init)).reshape(B, T, 2 * H_C * DH_C)

    qkv, conv_new = _causal_conv(d_qkv, conv0, p['gdn_conv'][l])
    qkv = qkv.astype(f32)
    dq, dk, dv = jnp.split(qkv, [H_D * DK_D, 2 * H_D * DK_D], axis=-1)
    dq = _l2norm(dq.reshape(B, T, H_D, DK_D)) * (DK_D ** -0.5)
    dk = _l2norm(dk.reshape(B, T, H_D, DK_D))
    dv = dv.reshape(B, T, H_D, DV_D)
    beta = jax.nn.sigmoid(d_beta.astype(f32))
    logg = -jnp.exp(p['gdn_a_log'][l].astype(f32)) * jax.nn.softplus(d_a.astype(f32) + p['gdn_dt_bias'][l].astype(f32))
    o_d, s_d = _gdn_chunked(dq, dk, dv, beta, logg, s0_d.astype(f32))
    o_d = (_rms_norm(o_d, p['gdn_norm'][l]) * jax.nn.silu(d_g.astype(f32).reshape(B, T, H_D, DV_D))).reshape(B, T, H_D * DV_D)

    o = jnp.concatenate([o_a, o_b, o_c, o_d], axis=-1).astype(x.dtype)
    x = x + gt1 * jnp.einsum('btm,md->btd', o, p['w_out'][l])

    h2 = _rms_norm(x, p['g_norm2'][l]) * (1.0 + sc2) + sh2
    ff = jnp.square(jax.nn.relu(jnp.einsum('btd,df->btf', h2, p['w_up'][l])))
    x = x + gt2 * jnp.einsum('btf,fd->btd', ff, p['w_down'][l])
    return x, (kh, vh, hr, hi, s_b, s_d, conv_new)


def _trunk(x, c, p, past):
    sm = jax.nn.softmax(p['hgrn_lb_logits'].astype(jnp.float32), axis=0)
    lb_all = jnp.cumsum(sm, axis=0)
    lb_all = lb_all - lb_all[0:1]
    cols = [[] for _ in range(7)]
    for l in range(DEPTH):
        x, st = _layer(l, x, c, lb_all[l], p, past)
        for j in range(7):
            cols[j].append(st[j])
    y = _rms_norm(x, p['g_final'])
    return y, [jnp.stack(col, axis=0) for col in cols]


def setup_inputs(seed: int = 0) -> dict:
    key = jax.random.key(seed)
    ks = iter(jax.random.split(key, 64))
    f32 = jnp.float32

    def nrm(shape, scale):
        return jax.random.normal(next(ks), shape, f32) * scale

    def gain(shape):
        return 1.0 + nrm(shape, 0.01)

    n_pages = PAST_LEN // PAGE_SIZE
    n_pool = (DEC_BATCH * n_pages * 5) // 4
    x_prompt = nrm((BATCH, SEQ, D_MODEL), 1.0)
    x_sample = nrm((DEC_BATCH, DEC_SEQ, D_MODEL), 1.0)
    c_prompt = nrm((BATCH, D_MODEL), 1.0)
    c_sample = nrm((DEC_BATCH, D_MODEL), 1.0)
    cache_k = nrm((DEPTH, n_pool, PAGE_SIZE, H_C, 2 * DH_C), 1.0)
    cache_v = nrm((DEPTH, n_pool, PAGE_SIZE, H_C, 2 * DH_C), 1.0)
    perm = jax.random.permutation(next(ks), n_pool)
    page_table = perm[:DEC_BATCH * n_pages].reshape(DEC_BATCH, n_pages).astype(jnp.int32)
    state_ssm_re = nrm((DEPTH, DEC_BATCH, G_A, P_A), 0.1)
    state_ssm_im = nrm((DEPTH, DEC_BATCH, G_A, P_A), 0.1)
    state_hgrn = nrm((DEPTH, DEC_BATCH, H_B, DK_B, DV_B), 0.5)
    state_gdn = nrm((DEPTH, DEC_BATCH, H_D, DK_D, DV_D), 0.3)
    state_gdn_conv = nrm((DEPTH, DEC_BATCH, CONV_W - 1, W_D_QKV), 1.0)

    w_ada = nrm((DEPTH, D_MODEL, 6 * D_MODEL), 0.5 * D_MODEL ** -0.5)
    b_ada = nrm((DEPTH, 6 * D_MODEL), 0.01)
    g_norm1 = gain((DEPTH, D_MODEL))
    w_in = nrm((DEPTH, D_MODEL, N_IN), D_MODEL ** -0.5)
    ssm_lam_re = -0.5 + nrm((DEPTH, G_A, P_A), 0.01)
    ssm_lam_im = jnp.pi * jnp.arange(P_A, dtype=f32)[None, None, :] + nrm((DEPTH, G_A, P_A), 0.01)
    ssm_log_dt = jax.random.uniform(next(ks), (DEPTH, G_A), f32, math.log(1e-3), math.log(1e-1))
    ssm_b_re = nrm((DEPTH, G_A, P_A, SSM_CH), (2 * SSM_CH) ** -0.5)
    ssm_b_im = nrm((DEPTH, G_A, P_A, SSM_CH), (2 * SSM_CH) ** -0.5)
    ssm_c_re = nrm((DEPTH, G_A, SSM_CH, P_A), 0.5)
    ssm_c_im = nrm((DEPTH, G_A, SSM_CH, P_A), 0.5)
    ssm_d = nrm((DEPTH, G_A, SSM_CH), 1.0)
    ssm_w_glu = nrm((DEPTH, W_A, W_A), W_A ** -0.5)
    ssm_b_glu = nrm((DEPTH, W_A), 0.01)
    hgrn_lb_logits = nrm((DEPTH, H_B * DK_B), 1.0)
    hgrn_norm = gain((DEPTH, DV_B))
    diff_lq1 = nrm((DEPTH, DH_C), 0.1)
    diff_lk1 = nrm((DEPTH, DH_C), 0.1)
    diff_lq2 = nrm((DEPTH, DH_C), 0.1)
    diff_lk2 = nrm((DEPTH, DH_C), 0.1)
    diff_norm = gain((DEPTH, 2 * DH_C))
    rel_bias = nrm((N_BUCKETS, H_C), 0.5)
    gdn_conv = nrm((DEPTH, CONV_W, W_D_QKV), CONV_W ** -0.5)
    gdn_a_log = jnp.log(jax.random.uniform(next(ks), (DEPTH, H_D), f32, 1.0, 16.0))
    dt = jnp.exp(jax.random.uniform(next(ks), (DEPTH, H_D), f32, math.log(1e-3), math.log(1e-1)))
    gdn_dt_bias = dt + jnp.log(-jnp.expm1(-dt))
    gdn_norm = gain((DEPTH, DV_D))
    w_out = nrm((DEPTH, MIX_WIDTH, D_MODEL), MIX_WIDTH ** -0.5)
    g_norm2 = gain((DEPTH, D_MODEL))
    w_up = nrm((DEPTH, D_MODEL, D_FF), D_MODEL ** -0.5)
    w_down = nrm((DEPTH, D_FF, D_MODEL), D_FF ** -0.5)
    g_final = gain((D_MODEL,))
    return {'x_prompt': x_prompt, 'x_sample': x_sample, 'c_prompt': c_prompt, 'c_sample': c_sample,
            'cache_k': cache_k, 'cache_v': cache_v, 'page_table': page_table,
            'state_ssm_re': state_ssm_re, 'state_ssm_im': state_ssm_im, 'state_hgrn': state_hgrn,
            'state_gdn': state_gdn, 'state_gdn_conv': state_gdn_conv,
            'w_ada': w_ada, 'b_ada': b_ada, 'g_norm1': g_norm1, 'w_in': w_in,
            'ssm_lam_re': ssm_lam_re, 'ssm_lam_im': ssm_lam_im, 'ssm_log_dt': ssm_log_dt,
            'ssm_b_re': ssm_b_re, 'ssm_b_im': ssm_b_im, 'ssm_c_re': ssm_c_re, 'ssm_c_im': ssm_c_im,
            'ssm_d': ssm_d, 'ssm_w_glu': ssm_w_glu, 'ssm_b_glu': ssm_b_glu,
            'hgrn_lb_logits': hgrn_lb_logits, 'hgrn_norm': hgrn_norm,
            'diff_lq1': diff_lq1, 'diff_lk1': diff_lk1, 'diff_lq2': diff_lq2, 'diff_lk2': diff_lk2,
            'diff_norm': diff_norm, 'rel_bias': rel_bias,
            'gdn_conv': gdn_conv, 'gdn_a_log': gdn_a_log, 'gdn_dt_bias': gdn_dt_bias, 'gdn_norm': gdn_norm,
            'w_out': w_out, 'g_norm2': g_norm2, 'w_up': w_up, 'w_down': w_down, 'g_final': g_final}


def reference(x_prompt, x_sample, c_prompt, c_sample, cache_k, cache_v, page_table,
              state_ssm_re, state_ssm_im, state_hgrn, state_gdn, state_gdn_conv,
              w_ada, b_ada, g_norm1, w_in,
              ssm_lam_re, ssm_lam_im, ssm_log_dt, ssm_b_re, ssm_b_im, ssm_c_re, ssm_c_im,
              ssm_d, ssm_w_glu, ssm_b_glu,
              hgrn_lb_logits, hgrn_norm,
              diff_lq1, diff_lk1, diff_lq2, diff_lk2, diff_norm, rel_bias,
              gdn_conv, gdn_a_log, gdn_dt_bias, gdn_norm,
              w_out, g_norm2, w_up, w_down, g_final):
    p = dict(w_ada=w_ada, b_ada=b_ada, g_norm1=g_norm1, w_in=w_in,
             ssm_lam_re=ssm_lam_re, ssm_lam_im=ssm_lam_im, ssm_log_dt=ssm_log_dt,
             ssm_b_re=ssm_b_re, ssm_b_im=ssm_b_im, ssm_c_re=ssm_c_re, ssm_c_im=ssm_c_im,
             ssm_d=ssm_d, ssm_w_glu=ssm_w_glu, ssm_b_glu=ssm_b_glu,
             hgrn_lb_logits=hgrn_lb_logits, hgrn_norm=hgrn_norm,
             diff_lq1=diff_lq1, diff_lk1=diff_lk1, diff_lq2=diff_lq2, diff_lk2=diff_lk2,
             diff_norm=diff_norm, rel_bias=rel_bias,
             gdn_conv=gdn_conv, gdn_a_log=gdn_a_log, gdn_dt_bias=gdn_dt_bias, gdn_norm=gdn_norm,
             w_out=w_out, g_norm2=g_norm2, w_up=w_up, w_down=w_down, g_final=g_final)
    y_prompt, sp = _trunk(x_prompt, c_prompt, p, None)
    past = dict(cache_k=cache_k, cache_v=cache_v, page_table=page_table,
                ssm_re=state_ssm_re, ssm_im=state_ssm_im, hgrn=state_hgrn,
                gdn=state_gdn, conv=state_gdn_conv)
    y_sample, ss = _trunk(x_sample, c_sample, p, past)
    return (y_prompt, y_sample,
            sp[0], sp[1], sp[2], sp[3], sp[4], sp[5], sp[6],
            ss[0], ss[1], ss[2], ss[3], ss[4], ss[5], ss[6])
```

```python
import functools
import math

import jax
import jax.numpy as jnp
from jax import lax
from jax.experimental import pallas as pl
from jax.experimental.pallas import tpu as pltpu

D_MODEL = 4096
DEPTH = 4
PAGE_SIZE = 128
GROUP_WIDTH = D_MODEL // 4
SSM_CH = 16
G_A = GROUP_WIDTH // SSM_CH
P_A = 64
W_A = G_A * SSM_CH
H_B = 8
DK_B = 128
DV_B = GROUP_WIDTH // H_B
H_C = 4
DH_C = GROUP_WIDTH // (2 * H_C)
N_BUCKETS = 32
MAX_DISTANCE = 128
Q_BLOCK = 128
H_D = 8
DK_D = 128
DV_D = GROUP_WIDTH // H_D
CONV_W = 4
W_D_QKV = H_D * (2 * DK_D + DV_D)
CHUNK = 64
D_FF = 4 * D_MODEL
N_IN = W_A + 2 * H_B * DK_B + 2 * H_B * DV_B + 6 * H_C * DH_C + W_D_QKV + H_D * DV_D + 2 * H_D
N_IN_MAIN = (N_IN // 128) * 128
N_TAIL = N_IN - N_IN_MAIN
EPS = 1e-6

VMEM_LIMIT_BYTES = 56 * 1024 * 1024


def _mm_body(a_ref, w_ref, *rest, nk, epilogue):
    *extra_refs, o_ref, acc_ref = rest
    k = pl.program_id(2)

    @pl.when(k == 0)
    def _():
        acc_ref[...] = jnp.zeros_like(acc_ref)

    acc_ref[...] += jnp.dot(a_ref[...], w_ref[...].astype(jnp.bfloat16),
                            preferred_element_type=jnp.float32)

    @pl.when(k == nk - 1)
    def _():
        acc = acc_ref[...]
        if epilogue == "none":
            out = acc
        elif epilogue == "bias":
            out = acc + extra_refs[0][...]
        elif epilogue == "relu2":
            r = jnp.maximum(acc, 0.0)
            out = r * r
        elif epilogue == "gated_residual":
            out = extra_refs[0][...] + extra_refs[1][...] * acc
        else:
            raise ValueError(epilogue)
        o_ref[...] = out.astype(o_ref.dtype)


def _matmul(a, w, layer, n_cols, *, tm, tn, tk, out_dtype, epilogue="none",
            extras=(), extra_specs=()):
    m, k_dim = a.shape
    assert w.shape[1] == k_dim and m % tm == 0 and n_cols % tn == 0 and k_dim % tk == 0
    nk = k_dim // tk
    grid = (m // tm, n_cols // tn, nk)
    in_specs = [
        pl.BlockSpec((tm, tk), lambda i, j, k: (i, k)),
        pl.BlockSpec((None, tk, tn), lambda i, j, k: (layer, k, j)),
    ] + list(extra_specs)
    return pl.pallas_call(
        functools.partial(_mm_body, nk=nk, epilogue=epilogue),
        grid=grid,
        in_specs=in_specs,
        out_specs=pl.BlockSpec((tm, tn), lambda i, j, k: (i, j)),
        out_shape=jax.ShapeDtypeStruct((m, n_cols), out_dtype),
        scratch_shapes=[pltpu.VMEM((tm, tn), jnp.float32)],
        compiler_params=pltpu.CompilerParams(
            dimension_semantics=("parallel", "parallel", "arbitrary"),
            vmem_limit_bytes=VMEM_LIMIT_BYTES),
    )(a, w, *extras)


def _tiles(m):
    if m >= 1024:
        return 1024, 1024, 512
    return m, 2048, 1024


def _rms_norm(x, g):
    xf = x.astype(jnp.float32)
    y = xf * lax.rsqrt(jnp.mean(xf * xf, axis=-1, keepdims=True) + EPS)
    return (y * g.astype(jnp.float32)).astype(x.dtype)


def _l2norm(x):
    return x * lax.rsqrt(jnp.sum(x * x, axis=-1, keepdims=True) + EPS)


def _complex_affine_combine(e1, e2):
    a1r, a1i, b1r, b1i = e1
    a2r, a2i, b2r, b2i = e2
    return (a2r * a1r - a2i * a1i, a2r * a1i + a2i * a1r,
            a2r * b1r - a2i * b1i + b2r, a2r * b1i + a2i * b1r + b2i)


def _s5(u, lam_re, lam_im, log_dt, b_re, b_im, c_re, c_im, d_skip, h0_re, h0_im):
    f32 = jnp.float32
    B, T, _ = u.shape
    uf = u.astype(f32).reshape(B, T, G_A, SSM_CH)
    lr, li = lam_re.astype(f32), lam_im.astype(f32)
    dt = jnp.exp(log_dt.astype(f32))[:, None]
    mag = jnp.exp(lr * dt)
    ab_re, ab_im = mag * jnp.cos(li * dt), mag * jnp.sin(li * dt)
    den = lr * lr + li * li
    fr = ((ab_re - 1.0) * lr + ab_im * li) / den
    fi = (ab_im * lr - (ab_re - 1.0) * li) / den
    br, bi = b_re.astype(f32), b_im.astype(f32)
    bb_re = fr[..., None] * br - fi[..., None] * bi
    bb_im = fr[..., None] * bi + fi[..., None] * br
    bu_re = jnp.einsum('btgc,gpc->btgp', uf, bb_re)
    bu_im = jnp.einsum('btgc,gpc->btgp', uf, bb_im)
    a_re = jnp.broadcast_to(ab_re, bu_re.shape)
    a_im = jnp.broadcast_to(ab_im, bu_im.shape)
    acc_re, acc_im, s_re, s_im = lax.associative_scan(
        _complex_affine_combine, (a_re, a_im, bu_re, bu_im), axis=1)
    h0r = h0_re.astype(f32)[:, None]
    h0i = h0_im.astype(f32)[:, None]
    hr = s_re + acc_re * h0r - acc_im * h0i
    hi = s_im + acc_re * h0i + acc_im * h0r
    y = (jnp.einsum('gcp,btgp->btgc', c_re.astype(f32), hr)
         - jnp.einsum('gcp,btgp->btgc', c_im.astype(f32), hi)
         + d_skip.astype(f32) * uf).reshape(B, T, W_A)
    return jax.nn.gelu(y), hr[:, -1], hi[:, -1]


def _to_chunks(a, L, n):
    B, T = a.shape[:2]
    a = jnp.pad(a, [(0, 0), (0, n * L - T)] + [(0, 0)] * (a.ndim - 2))
    a = a.reshape((B, n, L) + a.shape[2:])
    a = jnp.moveaxis(a, 1, 0)
    return jnp.swapaxes(a, 2, 3)


def _from_chunks(o, T):
    n, B, H, L = o.shape[:4]
    o = jnp.moveaxis(jnp.swapaxes(o, 2, 3), 0, 1)
    return o.reshape((B, n * L) + o.shape[3:])[:, :T]


def _gla_chunked(q, k, v, logf, S0):
    T = q.shape[1]
    L = min(CHUNK, T)
    n = -(-T // L)
    qc, kc, vc, gc = (_to_chunks(a, L, n) for a in (q, k, v, logf))
    tri = jnp.tril(jnp.ones((L, L), dtype=bool))

    def step(S, inp):
        qi, ki, vi, gi = inp
        b = jnp.cumsum(gi, axis=2)
        dec = jnp.exp(jnp.where(tri[:, :, None], b[:, :, :, None, :] - b[:, :, None, :, :], -jnp.inf))
        att = jnp.sum(qi[:, :, :, None, :] * ki[:, :, None, :, :] * dec, axis=-1)
        o = (jnp.einsum('bhtk,bhkv->bhtv', qi * jnp.exp(b), S)
             + jnp.einsum('bhts,bhsv->bhtv', att, vi))
        b_last = b[:, :, -1:, :]
        S_new = (jnp.exp(b_last[:, :, 0, :])[..., None] * S
                 + jnp.einsum('bhlk,bhlv->bhkv', ki * jnp.exp(b_last - b), vi))
        return S_new, o

    S, o = lax.scan(step, S0, (qc, kc, vc, gc))
    return _from_chunks(o, T), S


def _gdn_chunked(q, k, v, beta, logg, S0):
    T = q.shape[1]
    V = v.shape[-1]
    L = min(CHUNK, T)
    n = -(-T // L)
    qc, kc, vc, bc, gc = (_to_chunks(a, L, n) for a in (q, k, v, beta, logg))
    tri = jnp.tril(jnp.ones((L, L), dtype=bool))
    strict = jnp.tril(jnp.ones((L, L), dtype=bool), -1)
    eye = jnp.eye(L, dtype=jnp.float32)

    def step(S, inp):
        qi, ki, vi, bi, gi = inp
        gcum = jnp.cumsum(gi, axis=-1)
        ldec = jnp.exp(jnp.where(tri, gcum[..., :, None] - gcum[..., None, :], -jnp.inf))
        kb = ki * bi[..., None]
        m = jnp.where(strict, jnp.einsum('bhtk,bhsk->bhts', kb, ki) * ldec, 0.0)
        rhs = jnp.concatenate([vi * bi[..., None], kb * jnp.exp(gcum)[..., None]], axis=-1)
        sol = lax.linalg.triangular_solve(eye + m, rhs, left_side=True, lower=True, unit_diagonal=True)
        u, w = sol[..., :V], sol[..., V:]
        v_new = u - jnp.einsum('bhlk,bhkv->bhlv', w, S)
        att = jnp.einsum('bhtk,bhsk->bhts', qi, ki) * ldec
        o = (jnp.einsum('bhlk,bhkv->bhlv', qi * jnp.exp(gcum)[..., None], S)
             + jnp.einsum('bhts,bhsv->bhtv', att, v_new))
        g_last = gcum[..., -1]
        S_new = (jnp.exp(g_last)[..., None, None] * S
                 + jnp.einsum('bhlk,bhlv->bhkv', ki * jnp.exp(g_last[..., None] - gcum)[..., None], v_new))
        return S_new, o

    S, o = lax.scan(step, S0, (qc, kc, vc, bc, gc))
    return _from_chunks(o, T), S


def _causal_conv(x, buf, w):
    T = x.shape[1]
    xx = jnp.concatenate([buf.astype(x.dtype), x], axis=1)
    y = xx[:, 0:T] * w[0]
    for j in range(1, CONV_W):
        y = y + xx[:, j:j + T] * w[j]
    return jax.nn.silu(y), xx[:, T:]


def _t5_bucket(qpos, kpos):
    n = jnp.maximum(qpos[:, None] - kpos[None, :], 0)
    max_exact = N_BUCKETS // 2
    nf = jnp.maximum(n, 1).astype(jnp.float32)
    large = max_exact + (jnp.log(nf / max_exact) / math.log(MAX_DISTANCE / max_exact)
                         * (N_BUCKETS - max_exact)).astype(jnp.int32)
    large = jnp.minimum(large, N_BUCKETS - 1)
    return jnp.where(n < max_exact, n, large)


def _diff_attn_core(q, k, v, qpos, kpos, lam, rel_table):
    s = jnp.einsum('bqhcd,bkhcd->bchqk', q, k) * (DH_C ** -0.5)
    bias = jnp.transpose(rel_table.astype(jnp.float32)[_t5_bucket(qpos, kpos)], (2, 0, 1))
    causal = kpos[None, :] <= qpos[:, None]
    s = jnp.where(causal, s + bias, -jnp.inf)
    p = jax.nn.softmax(s, axis=-1)
    a = p[:, 0] - lam * p[:, 1]
    return jnp.einsum('bhqk,bkhe->bqhe', a, v)


def _diff_attn_prompt(q, k, v, lam, rel_table):
    B, T = q.shape[:2]
    qb = min(Q_BLOCK, T)
    nb = T // qb
    q_blocks = jnp.moveaxis(q.reshape((B, nb, qb) + q.shape[2:]), 1, 0)
    qpos = jnp.arange(T, dtype=jnp.int32).reshape(nb, qb)
    kpos = jnp.arange(T, dtype=jnp.int32)
    out = lax.map(lambda blk: _diff_attn_core(blk[0], k, v, blk[1], kpos, lam, rel_table), (q_blocks, qpos))
    return jnp.moveaxis(out, 0, 1).reshape((B, T) + out.shape[3:])


def _split_main():
    sizes = [W_A, H_B * DK_B, H_B * DK_B, H_B * DV_B, H_B * DV_B,
             2 * H_C * DH_C, 2 * H_C * DH_C, 2 * H_C * DH_C, W_D_QKV, H_D * DV_D]
    pts, acc = [], 0
    for s in sizes[:-1]:
        acc += s
        pts.append(acc)
    return pts


def _layer(l, x, mod, lb, p, past):
    f32 = jnp.float32
    bf16 = jnp.bfloat16
    B, T, _ = x.shape
    M = B * T
    sh1, sc1, gt1, sh2, sc2, gt2 = jnp.split(mod[:, None, :], 6, axis=-1)
    Mp = max(M, 16)
    tm, tn, tk = _tiles(Mp)

    def pad_rows(a):
        return a if Mp == M else jnp.pad(a, ((0, Mp - M), (0, 0)))

    h = _rms_norm(x, p['g_norm1'][l]) * (1.0 + sc1) + sh1
    h2d = pad_rows(h.reshape(M, D_MODEL).astype(bf16))
    z = _matmul(h2d, p['w_in'], l, N_IN_MAIN, tm=tm, tn=tn, tk=tk, out_dtype=f32)
    zt = _matmul(h2d, p['w_in_tail'], l, 128, tm=tm, tn=128, tk=tk, out_dtype=f32)
    z = z[:M].reshape(B, T, N_IN_MAIN)
    zt = zt[:M].reshape(B, T, 128)
    (a_u, b_q, b_f, b_i, b_g, c_q, c_k, c_v, d_qkv, d_g) = jnp.split(z, _split_main(), axis=-1)
    d_beta, d_a = zt[..., :H_D], zt[..., H_D:2 * H_D]

    if past is None:
        h0_re = jnp.zeros((B, G_A, P_A), f32)
        h0_im = jnp.zeros((B, G_A, P_A), f32)
        s0_b = jnp.zeros((B, H_B, DK_B, DV_B), f32)
        s0_d = jnp.zeros((B, H_D, DK_D, DV_D), f32)
        conv0 = jnp.zeros((B, CONV_W - 1, W_D_QKV), z.dtype)
    else:
        h0_re, h0_im = past['ssm_re'][l], past['ssm_im'][l]
        s0_b, s0_d, conv0 = past['hgrn'][l], past['gdn'][l], past['conv'][l]

    zg, hr, hi = _s5(a_u, p['ssm_lam_re'][l], p['ssm_lam_im'][l], p['ssm_log_dt'][l],
                     p['ssm_b_re'][l], p['ssm_b_im'][l], p['ssm_c_re'][l], p['ssm_c_im'][l],
                     p['ssm_d'][l], h0_re, h0_im)
    glu_tn = 1024
    glu = _matmul(pad_rows(zg.reshape(M, W_A).astype(bf16)), p['ssm_w_glu'], l, W_A,
                  tm=tm, tn=glu_tn, tk=min(tk, W_A), out_dtype=f32, epilogue="bias",
                  extras=(p['ssm_b_glu'][l][None, :],),
                  extra_specs=(pl.BlockSpec((1, glu_tn), lambda i, j, k: (0, j)),))
    o_a = zg * jax.nn.sigmoid(glu[:M].reshape(B, T, W_A))

    qb_ = jax.nn.silu(b_q.astype(f32)).reshape(B, T, H_B, DK_B) * (DK_B ** -0.5)
    fl = b_f.astype(f32).reshape(B, T, H_B, DK_B)
    lbh = lb.reshape(H_B, DK_B)
    logf = jnp.logaddexp(jnp.log(lbh), jnp.log1p(-lbh) + jax.nn.log_sigmoid(fl))
    kb_ = (1.0 - lbh) * jax.nn.sigmoid(-fl)
    vb_ = b_i.astype(f32).reshape(B, T, H_B, DV_B)
    o_b, s_b = _gla_chunked(qb_, kb_, vb_, logf, s0_b.astype(f32))
    o_b = (_rms_norm(o_b, p['hgrn_norm'][l]) * jax.nn.silu(b_g.astype(f32).reshape(B, T, H_B, DV_B))).reshape(B, T, H_B * DV_B)

    lam_init = 0.8 - 0.6 * math.exp(-0.3 * l)
    lam = (jnp.exp(jnp.sum(p['diff_lq1'][l].astype(f32) * p['diff_lk1'][l].astype(f32)))
           - jnp.exp(jnp.sum(p['diff_lq2'][l].astype(f32) * p['diff_lk2'][l].astype(f32))) + lam_init)
    qc_ = c_q.astype(f32).reshape(B, T, H_C, 2, DH_C)
    kh = c_k.reshape(B, T, H_C, 2 * DH_C)
    vh = c_v.reshape(B, T, H_C, 2 * DH_C)
    if past is None:
        o_c = _diff_attn_prompt(qc_, kh.astype(f32).reshape(B, T, H_C, 2, DH_C), vh.astype(f32), lam, p['rel_bias'])
    else:
        kp = past['cache_k'][l, past['page_table']].reshape(B, -1, H_C, 2 * DH_C)
        vp = past['cache_v'][l, past['page_table']].reshape(B, -1, H_C, 2 * DH_C)
        past_len = kp.shape[1]
        k_all = jnp.concatenate([kp.astype(f32), kh.astype(f32)], axis=1).reshape(B, past_len + T, H_C, 2, DH_C)
        v_all = jnp.concatenate([vp.astype(f32), vh.astype(f32)], axis=1)
        qpos = past_len + jnp.arange(T, dtype=jnp.int32)
        kpos = jnp.arange(past_len + T, dtype=jnp.int32)
        o_c = _diff_attn_core(qc_, k_all, v_all, qpos, kpos, lam, p['rel_bias'])
    o_c = (_rms_norm(o_c, p['diff_norm'][l]) * (1.0 - lam_init)).reshape(B, T, 2 * H_C * DH_C)

    qkv, conv_new = _causal_conv(d_qkv, conv0, p['gdn_conv'][l])
    qkv = qkv.astype(f32)
    dq, dk, dv = jnp.split(qkv, [H_D * DK_D, 2 * H_D * DK_D], axis=-1)
    dq = _l2norm(dq.reshape(B, T, H_D, DK_D)) * (DK_D ** -0.5)
    dk = _l2norm(dk.reshape(B, T, H_D, DK_D))
    dv = dv.reshape(B, T, H_D, DV_D)
    beta = jax.nn.sigmoid(d_beta.astype(f32))
    logg = -jnp.exp(p['gdn_a_log'][l].astype(f32)) * jax.nn.softplus(d_a.astype(f32) + p['gdn_dt_bias'][l].astype(f32))
    o_d, s_d = _gdn_chunked(dq, dk, dv, beta, logg, s0_d.astype(f32))
    o_d = (_rms_norm(o_d, p['gdn_norm'][l]) * jax.nn.silu(d_g.astype(f32).reshape(B, T, H_D, DV_D))).reshape(B, T, H_D * DV_D)

    o = pad_rows(jnp.concatenate([o_a, o_b, o_c, o_d], axis=-1).reshape(M, D_MODEL).astype(bf16))
    x2d = pad_rows(x.reshape(M, D_MODEL))

    def gate_rows(g):
        if T < tm:
            return pad_rows(jnp.broadcast_to(g, (B, T, D_MODEL)).reshape(M, D_MODEL)), \
                pl.BlockSpec((tm, tn), lambda i, j, k: (i, j))
        per = T // tm
        return g.reshape(B, 1, D_MODEL), \
            pl.BlockSpec((None, 1, tn), lambda i, j, k: (i // per, 0, j))

    g1, g1_spec = gate_rows(gt1)
    x2d = _matmul(o, p['w_out'], l, D_MODEL, tm=tm, tn=tn, tk=tk, out_dtype=f32,
                  epilogue="gated_residual", extras=(x2d, g1),
                  extra_specs=(pl.BlockSpec((tm, tn), lambda i, j, k: (i, j)), g1_spec))

    xm = x2d[:M].reshape(B, T, D_MODEL)
    h2 = pad_rows((_rms_norm(xm, p['g_norm2'][l]) * (1.0 + sc2) + sh2).reshape(M, D_MODEL).astype(bf16))
    ff = _matmul(h2, p['w_up'], l, D_FF, tm=tm, tn=tn, tk=tk, out_dtype=bf16, epilogue="relu2")
    g2, g2_spec = gate_rows(gt2)
    x2d = _matmul(ff, p['w_down'], l, D_MODEL, tm=tm, tn=tn, tk=tk, out_dtype=f32,
                  epilogue="gated_residual", extras=(x2d, g2),
                  extra_specs=(pl.BlockSpec((tm, tn), lambda i, j, k: (i, j)), g2_spec))
    return x2d[:M].reshape(B, T, D_MODEL), (kh, vh, hr, hi, s_b, s_d, conv_new)


def _trunk(x, mods, p, past):
    sm = jax.nn.softmax(p['hgrn_lb_logits'].astype(jnp.float32), axis=0)
    lb_all = jnp.cumsum(sm, axis=0)
    lb_all = lb_all - lb_all[0:1]
    cols = [[] for _ in range(7)]
    for l in range(DEPTH):
        x, st = _layer(l, x, mods[l], lb_all[l], p, past)
        for j in range(7):
            cols[j].append(st[j])
    y = _rms_norm(x, p['g_final'])
    return y, [jnp.stack(col, axis=0) for col in cols]


def kernel(x_prompt, x_sample, c_prompt, c_sample, cache_k, cache_v, page_table, state_ssm_re, state_ssm_im, state_hgrn, state_gdn, state_gdn_conv, w_ada, b_ada, g_norm1, w_in, ssm_lam_re, ssm_lam_im, ssm_log_dt, ssm_b_re, ssm_b_im, ssm_c_re, ssm_c_im, ssm_d, ssm_w_glu, ssm_b_glu, hgrn_lb_logits, hgrn_norm, diff_lq1, diff_lk1, diff_lq2, diff_lk2, diff_norm, rel_bias, gdn_conv, gdn_a_log, gdn_dt_bias, gdn_norm, w_out, g_norm2, w_up, w_down, g_final):
    w_in_tail = jnp.pad(w_in[:, :, N_IN_MAIN:], ((0, 0), (0, 0), (0, 128 - N_TAIL)))
    p = dict(w_ada=w_ada, b_ada=b_ada, g_norm1=g_norm1, w_in=w_in, w_in_tail=w_in_tail,
             ssm_lam_re=ssm_lam_re, ssm_lam_im=ssm_lam_im, ssm_log_dt=ssm_log_dt,
             ssm_b_re=ssm_b_re, ssm_b_im=ssm_b_im, ssm_c_re=ssm_c_re, ssm_c_im=ssm_c_im,
             ssm_d=ssm_d, ssm_w_glu=ssm_w_glu, ssm_b_glu=ssm_b_glu,
             hgrn_lb_logits=hgrn_lb_logits, hgrn_norm=hgrn_norm,
             diff_lq1=diff_lq1, diff_lk1=diff_lk1, diff_lq2=diff_lq2, diff_lk2=diff_lk2,
             diff_norm=diff_norm, rel_bias=rel_bias,
             gdn_conv=gdn_conv, gdn_a_log=gdn_a_log, gdn_dt_bias=gdn_dt_bias, gdn_norm=gdn_norm,
             w_out=w_out, g_norm2=g_norm2, w_up=w_up, w_down=w_down, g_final=g_final)

    nb_p, nb_s = c_prompt.shape[0], c_sample.shape[0]
    c_all = jnp.concatenate([c_prompt, c_sample], axis=0)
    rows = 16
    cs = jnp.pad(jax.nn.silu(c_all), ((0, rows - nb_p - nb_s), (0, 0))).astype(jnp.bfloat16)
    mods_p, mods_s = [], []
    for l in range(DEPTH):
        m = _matmul(cs, w_ada, l, 6 * D_MODEL, tm=rows, tn=2048, tk=1024, out_dtype=jnp.float32,
                    epilogue="bias", extras=(b_ada[l][None, :],),
                    extra_specs=(pl.BlockSpec((1, 2048), lambda i, j, k: (0, j)),))
        mods_p.append(m[:nb_p])
        mods_s.append(m[nb_p:nb_p + nb_s])

    y_prompt, sp = _trunk(x_prompt, mods_p, p, None)

    past = dict(cache_k=cache_k, cache_v=cache_v, page_table=page_table,
                ssm_re=state_ssm_re, ssm_im=state_ssm_im, hgrn=state_hgrn,
                gdn=state_gdn, conv=state_gdn_conv)
    y_sample, ss = _trunk(x_sample, mods_s, p, past)
    return (y_prompt, y_sample,
            sp[0], sp[1], sp[2], sp[3], sp[4], sp[5], sp[6],
            ss[0], ss[1], ss[2], ss[3], ss[4], ss[5], ss[6])
```

```python
import functools
import math

import jax
import jax.numpy as jnp
from jax import lax
from jax.experimental import pallas as pl
from jax.experimental.pallas import tpu as pltpu

D_MODEL = 4096
DEPTH = 4
PAGE_SIZE = 128
GROUP_WIDTH = D_MODEL // 4
SSM_CH = 16
G_A = GROUP_WIDTH // SSM_CH
P_A = 64
W_A = G_A * SSM_CH
H_B = 8
DK_B = 128
DV_B = GROUP_WIDTH // H_B
H_C = 4
DH_C = GROUP_WIDTH // (2 * H_C)
N_BUCKETS = 32
MAX_DISTANCE = 128
Q_BLOCK = 128
H_D = 8
DK_D = 128
DV_D = GROUP_WIDTH // H_D
CONV_W = 4
W_D_QKV = H_D * (2 * DK_D + DV_D)
CHUNK = 64
D_FF = 4 * D_MODEL
N_IN = W_A + 2 * H_B * DK_B + 2 * H_B * DV_B + 6 * H_C * DH_C + W_D_QKV + H_D * DV_D + 2 * H_D
N_IN_MAIN = (N_IN // 128) * 128
N_TAIL = N_IN - N_IN_MAIN
EPS = 1e-6

VMEM_LIMIT_BYTES = 56 * 1024 * 1024


def _mm_body(a_ref, w_ref, *rest, nk, epilogue):
    *extra_refs, o_ref, acc_ref = rest
    k = pl.program_id(2)

    @pl.when(k == 0)
    def _():
        acc_ref[...] = jnp.zeros_like(acc_ref)

    acc_ref[...] += jnp.dot(a_ref[...], w_ref[...].astype(jnp.bfloat16),
                            preferred_element_type=jnp.float32)

    @pl.when(k == nk - 1)
    def _():
        acc = acc_ref[...]
        if epilogue == "none":
            out = acc
        elif epilogue == "bias":
            out = acc + extra_refs[0][...]
        elif epilogue == "relu2":
            r = jnp.maximum(acc, 0.0)
            out = r * r
        elif epilogue == "gated_residual":
            out = extra_refs[0][...] + extra_refs[1][...] * acc
        else:
            raise ValueError(epilogue)
        o_ref[...] = out.astype(o_ref.dtype)


def _matmul(a, w, layer, n_cols, *, tm, tn, tk, out_dtype, epilogue="none",
            extras=(), extra_specs=()):
    m, k_dim = a.shape
    assert w.shape[1] == k_dim and m % tm == 0 and n_cols % tn == 0 and k_dim % tk == 0
    nk = k_dim // tk
    grid = (m // tm, n_cols // tn, nk)
    in_specs = [
        pl.BlockSpec((tm, tk), lambda i, j, k: (i, k)),
        pl.BlockSpec((None, tk, tn), lambda i, j, k: (layer, k, j)),
    ] + list(extra_specs)
    return pl.pallas_call(
        functools.partial(_mm_body, nk=nk, epilogue=epilogue),
        grid=grid,
        in_specs=in_specs,
        out_specs=pl.BlockSpec((tm, tn), lambda i, j, k: (i, j)),
        out_shape=jax.ShapeDtypeStruct((m, n_cols), out_dtype),
        scratch_shapes=[pltpu.VMEM((tm, tn), jnp.float32)],
        compiler_params=pltpu.CompilerParams(
            dimension_semantics=("parallel", "parallel", "arbitrary"),
            vmem_limit_bytes=VMEM_LIMIT_BYTES),
    )(a, w, *extras)


def _tiles(m):
    if m >= 1024:
        return 1024, 1024, 512
    return m, 2048, 1024


def _split_bf16(x):
    hi = x.astype(jnp.bfloat16)
    lo = (x - hi.astype(jnp.float32)).astype(jnp.bfloat16)
    return hi, lo


def _dot3(a, b, dims=(((1,), (0,)), ((), ()))):
    ah, al = _split_bf16(a)
    bh, bl = _split_bf16(b)

    def d(x, y):
        return lax.dot_general(x, y, dims, preferred_element_type=jnp.float32)

    return d(ah, bh) + (d(ah, bl) + d(al, bh))


_NT = (((1,), (1,)), ((), ()))
_TN = (((0,), (0,)), ((), ()))


S5_L = 16


def _s5_tables(lam_re, lam_im, log_dt, b_re, b_im, c_re, c_im, d_skip, n_chunks, L=S5_L):
    f32 = jnp.float32
    hp = lax.Precision.HIGHEST
    G, P, C = G_A, P_A, SSM_CH
    lr, li = lam_re.astype(f32), lam_im.astype(f32)
    dt = jnp.exp(log_dt.astype(f32))[:, None]
    mag = jnp.exp(lr * dt)
    ab_re, ab_im = mag * jnp.cos(li * dt), mag * jnp.sin(li * dt)
    den = lr * lr + li * li
    fr = ((ab_re - 1.0) * lr + ab_im * li) / den
    fi = (ab_im * lr - (ab_re - 1.0) * li) / den
    br, bi = b_re.astype(f32), b_im.astype(f32)
    bb_re = fr[..., None] * br - fi[..., None] * bi
    bb_im = fr[..., None] * bi + fi[..., None] * br

    def powers(ns):
        nf = jnp.asarray(ns, f32)[None, :, None]
        m = jnp.exp(nf * (lr * dt)[:, None, :])
        ph = nf * (li * dt)[:, None, :]
        return m * jnp.cos(ph), m * jnp.sin(ph)

    ar, ai = powers(list(range(L + 1)))
    cr, ci = c_re.astype(f32), c_im.astype(f32)
    car = cr[:, None] * ar[:, :, None, :] - ci[:, None] * ai[:, :, None, :]
    cai = cr[:, None] * ai[:, :, None, :] + ci[:, None] * ar[:, :, None, :]
    kk = (jnp.einsum('gncp,gpd->gncd', car[:, :L], bb_re, precision=hp)
          - jnp.einsum('gncp,gpd->gncd', cai[:, :L], bb_im, precision=hp))
    tl = jnp.arange(L)
    lag = tl[:, None] - tl[None, :]
    kfull = kk[:, jnp.clip(lag, 0, L - 1)]
    kfull = jnp.where((lag >= 0)[None, :, :, None, None], kfull, 0.0)
    wt = jnp.transpose(kfull, (0, 2, 4, 1, 3)).reshape(G, L * C, L * C)
    arr, aii = ar[:, L - 1 - tl], ai[:, L - 1 - tl]
    bm_re = arr[:, :, None, :] * jnp.swapaxes(bb_re, 1, 2)[:, None] - aii[:, :, None, :] * jnp.swapaxes(bb_im, 1, 2)[:, None]
    bm_im = arr[:, :, None, :] * jnp.swapaxes(bb_im, 1, 2)[:, None] + aii[:, :, None, :] * jnp.swapaxes(bb_re, 1, 2)[:, None]
    bmt = jnp.concatenate([bm_re, bm_im], axis=-1).reshape(G, L * C, 2 * P)
    cm_re = jnp.transpose(car[:, 1:], (0, 3, 1, 2))
    cm_im = -jnp.transpose(cai[:, 1:], (0, 3, 1, 2))
    cmt = jnp.concatenate([cm_re, cm_im], axis=1).reshape(G, 2 * P, L * C)
    n_steps = max(1, (n_chunks - 1).bit_length())
    sr, si = powers([L * (1 << i) for i in range(n_steps)])
    ca = jnp.concatenate([sr, sr], axis=-1)
    cb = jnp.concatenate([-si, si], axis=-1)
    dtile = jnp.tile(d_skip.astype(f32), (1, L))[:, None, :]
    return wt, bmt, cmt, ca, cb, dtile, n_steps


def _s5_body(x_ref, wt_ref, bm_ref, cm_ref, ca_ref, cb_ref, d_ref, y_ref, st_ref, *,
             n_chunks, n_steps, batch):
    x = x_ref[...]
    rows = x.shape[0]
    e = _dot3(x, bm_ref[...])
    jidx = lax.broadcasted_iota(jnp.int32, (rows, 2 * P_A), 0) % n_chunks
    for i in range(n_steps):
        sh = 1 << i
        xs = jnp.where(jidx >= sh, pltpu.roll(e, sh, axis=0), 0.0)
        e = e + xs * ca_ref[i:i + 1, :] + pltpu.roll(xs, P_A, axis=1) * cb_ref[i:i + 1, :]
    hprev = jnp.where(jidx >= 1, pltpu.roll(e, 1, axis=0), 0.0)
    y_ref[...] = _dot3(x, wt_ref[...]) + _dot3(hprev, cm_ref[...]) + x * d_ref[...]
    for b in range(batch):
        r = b * n_chunks + n_chunks - 1
        st_ref[b:b + 1, :] = e[r:r + 1, :]


def _s5_prompt(a_u, tables):
    wt, bmt, cmt, ca, cb, dtile, n_steps = tables
    B, T, _ = a_u.shape
    L, G, C, P = S5_L, G_A, SSM_CH, P_A
    nc = T // L
    rows = B * nc
    x = a_u.reshape(B, nc, L, G, C).transpose(3, 0, 1, 2, 4).reshape(G, rows, L * C)
    full = lambda *shape: pl.BlockSpec((None,) + shape, lambda g: (g,) + (0,) * len(shape))
    y, st = pl.pallas_call(
        functools.partial(_s5_body, n_chunks=nc, n_steps=n_steps, batch=B),
        grid=(G,),
        in_specs=[full(rows, L * C), full(L * C, L * C), full(L * C, 2 * P), full(2 * P, L * C),
                  full(n_steps, 2 * P), full(n_steps, 2 * P), full(1, L * C)],
        out_specs=[full(rows, L * C), full(B, 2 * P)],
        out_shape=[jax.ShapeDtypeStruct((G, rows, L * C), jnp.float32),
                   jax.ShapeDtypeStruct((G, B, 2 * P), jnp.float32)],
        compiler_params=pltpu.CompilerParams(dimension_semantics=("parallel",)),
    )(x, wt, bmt, cmt, ca, cb, dtile)
    y = y.reshape(G, B, nc, L, C).transpose(1, 2, 3, 0, 4).reshape(B, T, G * C)
    st = jnp.swapaxes(st, 0, 1)
    return y, st[..., :P], st[..., P:]


def _s5_step_body(x_ref, h_ref, w_ref, bm_ref, cm_ref, ca_ref, cb_ref, d_ref, y_ref, st_ref):
    x = x_ref[...]
    h0 = h_ref[...]
    st_ref[...] = (_dot3(x, bm_ref[...]) + h0 * ca_ref[...]
                   + pltpu.roll(h0, P_A, axis=1) * cb_ref[...])
    y_ref[...] = _dot3(x, w_ref[...]) + _dot3(h0, cm_ref[...]) + x * d_ref[...]


def _s5_step(a_u, h0_re, h0_im, tables):
    wt, bmt, cmt, ca, cb, dtile, _ = tables
    B = a_u.shape[0]
    G, C, P = G_A, SSM_CH, P_A
    x = a_u.reshape(B, G, C).transpose(1, 0, 2)
    h0 = jnp.concatenate([h0_re, h0_im], axis=-1).transpose(1, 0, 2)
    full = lambda *shape: pl.BlockSpec((None,) + shape, lambda g: (g,) + (0,) * len(shape))
    y, st = pl.pallas_call(
        _s5_step_body,
        grid=(G,),
        in_specs=[full(B, C), full(B, 2 * P), full(C, C), full(C, 2 * P), full(2 * P, C),
                  full(1, 2 * P), full(1, 2 * P), full(1, C)],
        out_specs=[full(B, C), full(B, 2 * P)],
        out_shape=[jax.ShapeDtypeStruct((G, B, C), jnp.float32),
                   jax.ShapeDtypeStruct((G, B, 2 * P), jnp.float32)],
        compiler_params=pltpu.CompilerParams(dimension_semantics=("parallel",)),
    )(x, h0.astype(jnp.float32), wt, bmt, cmt, ca, cb, dtile)
    y = y.transpose(1, 0, 2).reshape(B, 1, G * C)
    st = jnp.swapaxes(st, 0, 1)
    return y, st[..., :P], st[..., P:]


def _gelu_tanh(x):
    c = math.sqrt(2.0 / math.pi)
    return 0.5 * x * (1.0 + jnp.tanh(c * (x + 0.044715 * (x * x * x))))


def _glu_body(y_ref, w_ref, b_ref, o_ref):
    zg = _gelu_tanh(y_ref[...])
    glu = jnp.dot(zg.astype(jnp.bfloat16), w_ref[...].astype(jnp.bfloat16),
                  preferred_element_type=jnp.float32) + b_ref[...]
    o_ref[...] = (zg * jax.nn.sigmoid(glu)).astype(o_ref.dtype)


def _s5_glu(y2d, w_glu, b_glu, layer, tm):
    m = y2d.shape[0]
    return pl.pallas_call(
        _glu_body,
        grid=(m // tm,),
        in_specs=[pl.BlockSpec((tm, W_A), lambda i: (i, 0)),
                  pl.BlockSpec((None, W_A, W_A), lambda i: (layer, 0, 0)),
                  pl.BlockSpec((None, 1, W_A), lambda i: (layer, 0, 0))],
        out_specs=pl.BlockSpec((tm, W_A), lambda i: (i, 0)),
        out_shape=jax.ShapeDtypeStruct((m, W_A), jnp.bfloat16),
        compiler_params=pltpu.CompilerParams(dimension_semantics=("parallel",),
                                             vmem_limit_bytes=VMEM_LIMIT_BYTES),
    )(y2d, w_glu, b_glu.reshape(DEPTH, 1, W_A))


ATT_T = 256
MASKED = -1e30
_CQ_BLK = (W_A + 2 * H_B * DK_B + 2 * H_B * DV_B) // (2 * DH_C)
_CK_BLK = _CQ_BLK + H_C
_CV_BLK = _CK_BLK + H_C


def _attn_bias_tiles(rel_bias):
    t = ATT_T
    assert t >= MAX_DISTANCE
    qpos = jnp.arange(3 * t, dtype=jnp.int32)
    kpos = jnp.arange(t, dtype=jnp.int32)
    bias = rel_bias.astype(jnp.float32)[_t5_bucket(qpos, kpos)]
    bias = jnp.where((kpos[None, :] <= qpos[:, None])[..., None], bias, MASKED)
    return bias.transpose(2, 0, 1).reshape(H_C, 3, t, t)


def _softmax_update(c, s, v_bf16, m_ref, l_ref, acc_ref):
    m_prev = m_ref[c]
    m_new = jnp.maximum(m_prev, jnp.max(s, axis=-1, keepdims=True))
    alpha = jnp.exp(m_prev - m_new)
    p = jnp.exp(s - m_new)
    l_ref[c] = alpha * l_ref[c] + jnp.sum(p, axis=-1, keepdims=True)
    acc_ref[c] = alpha * acc_ref[c] + jnp.dot(p.astype(jnp.bfloat16), v_bf16,
                                              preferred_element_type=jnp.float32)
    m_ref[c] = m_new


def _head_norm(o, g, out_scale):
    return o * lax.rsqrt(jnp.mean(o * o, axis=-1, keepdims=True) + EPS) * g * out_scale


def _attn_body(lam_ref, q_ref, k_ref, v_ref, bias_ref, g_ref, o_ref, m_ref, l_ref, acc_ref, *,
               out_scale):
    qi = pl.program_id(2)
    ki = pl.program_id(3)

    @pl.when(ki == 0)
    def _():
        m_ref[...] = jnp.full_like(m_ref, MASKED)
        l_ref[...] = jnp.zeros_like(l_ref)
        acc_ref[...] = jnp.zeros_like(acc_ref)

    @pl.when(ki <= qi)
    def _():
        q = q_ref[...].astype(jnp.bfloat16)
        k = k_ref[...].astype(jnp.bfloat16)
        v = v_ref[...].astype(jnp.bfloat16)
        bias = bias_ref[...]
        for c in range(2):
            s = lax.dot_general(q[:, c * DH_C:(c + 1) * DH_C], k[:, c * DH_C:(c + 1) * DH_C], _NT,
                                preferred_element_type=jnp.float32) * (DH_C ** -0.5) + bias
            _softmax_update(c, s, v, m_ref, l_ref, acc_ref)

    @pl.when(ki == qi)
    def _():
        o = acc_ref[0] / l_ref[0] - lam_ref[...] * (acc_ref[1] / l_ref[1])
        o_ref[...] = _head_norm(o, g_ref[...], out_scale).astype(o_ref.dtype)


def _diff_attn_prompt_pallas(z2d, batch, seq, lam, bias_tiles, g_norm, out_scale):
    t = ATT_T
    nq = seq // t
    w = 2 * DH_C
    grid = (batch, H_C, nq, nq)
    return pl.pallas_call(
        functools.partial(_attn_body, out_scale=out_scale),
        grid=grid,
        in_specs=[
            pl.BlockSpec((1, 1), lambda b, h, qi, ki: (0, 0)),
            pl.BlockSpec((t, w), lambda b, h, qi, ki: (b * nq + qi, _CQ_BLK + h)),
            pl.BlockSpec((t, w), lambda b, h, qi, ki: (b * nq + jnp.minimum(ki, qi), _CK_BLK + h)),
            pl.BlockSpec((t, w), lambda b, h, qi, ki: (b * nq + jnp.minimum(ki, qi), _CV_BLK + h)),
            pl.BlockSpec((None, None, t, t),
                         lambda b, h, qi, ki: (h, jnp.clip(qi - ki, 0, 2), 0, 0)),
            pl.BlockSpec((1, w), lambda b, h, qi, ki: (0, 0)),
        ],
        out_specs=pl.BlockSpec((t, w), lambda b, h, qi, ki: (b * nq + qi, h)),
        out_shape=jax.ShapeDtypeStruct((batch * seq, H_C * w), jnp.bfloat16),
        scratch_shapes=[pltpu.VMEM((2, t, 1), jnp.float32), pltpu.VMEM((2, t, 1), jnp.float32),
                        pltpu.VMEM((2, t, w), jnp.float32)],
        compiler_params=pltpu.CompilerParams(
            dimension_semantics=("parallel", "parallel", "parallel", "arbitrary")),
    )(lam.reshape(1, 1), z2d, z2d, z2d, bias_tiles, g_norm.reshape(1, w))


def _attn_step_body(pt_ref, lam_ref, q_ref, kn_ref, vn_ref, kc_ref, vc_ref, bias_ref, bias0_ref, g_ref,
                    o_ref, m_ref, l_ref, acc_ref, qrows_ref, *, n_pages, out_scale):
    del pt_ref
    p = pl.program_id(1)
    w = 2 * DH_C
    nrow = qrows_ref.shape[0]

    @pl.when(p == 0)
    def _():
        col = lax.broadcasted_iota(jnp.int32, (nrow, H_C * w), 1)
        row = lax.broadcasted_iota(jnp.int32, (nrow, H_C * w), 0)
        qrows = jnp.where(col // DH_C == row, q_ref[...], 0.0)
        qrows_ref[...] = qrows.astype(jnp.bfloat16)
        s_new = (jnp.sum(qrows * kn_ref[...], axis=-1, keepdims=True) * (DH_C ** -0.5)
                 + bias0_ref[...])
        m_ref[0] = s_new
        l_ref[0] = jnp.ones_like(s_new)
        acc_ref[0] = jnp.broadcast_to(vn_ref[...], (nrow, H_C * w))

    s = lax.dot_general(qrows_ref[...], kc_ref[...].astype(jnp.bfloat16), _NT,
                        preferred_element_type=jnp.float32) * (DH_C ** -0.5) + bias_ref[...]
    _softmax_update(0, s, vc_ref[...].astype(jnp.bfloat16), m_ref, l_ref, acc_ref)

    @pl.when(p == n_pages - 1)
    def _():
        a = acc_ref[0] / l_ref[0]
        for h in range(H_C):
            o = (a[2 * h:2 * h + 1, h * w:(h + 1) * w]
                 - lam_ref[...] * a[2 * h + 1:2 * h + 2, h * w:(h + 1) * w])
            o_ref[:, h * w:(h + 1) * w] = _head_norm(o, g_ref[...], out_scale)


def _diff_attn_step_pallas(q, k_new, v_new, cache_k, cache_v, page_table, layer, lam, rel_bias,
                           g_norm, out_scale):
    f32 = jnp.float32
    batch, n_pages = page_table.shape
    w = 2 * DH_C
    wide = H_C * w
    nrow = 16
    past_len = n_pages * PAGE_SIZE
    ck = cache_k.reshape(cache_k.shape[0], cache_k.shape[1], PAGE_SIZE, wide)
    cv = cache_v.reshape(cache_v.shape[0], cache_v.shape[1], PAGE_SIZE, wide)
    bucket = _t5_bucket(jnp.full((1,), past_len, jnp.int32), jnp.arange(past_len + 1, dtype=jnp.int32))[0]
    bias_h = rel_bias.astype(f32)[bucket].T
    bias_rows = jnp.pad(jnp.repeat(bias_h, 2, axis=0), ((0, nrow - 2 * H_C), (0, 0)))
    bias_past, bias0 = bias_rows[:, :past_len], bias_rows[:, past_len:]
    vec = lambda a: a.reshape(batch, 1, wide).astype(f32)
    row_spec = pl.BlockSpec((None, 1, wide), lambda b, p, pt: (b, 0, 0))
    return pl.pallas_call(
        functools.partial(_attn_step_body, n_pages=n_pages, out_scale=out_scale),
        grid_spec=pltpu.PrefetchScalarGridSpec(
            num_scalar_prefetch=1,
            grid=(batch, n_pages),
            in_specs=[
                pl.BlockSpec((1, 1), lambda b, p, pt: (0, 0)),
                row_spec, row_spec, row_spec,
                pl.BlockSpec((None, None, PAGE_SIZE, wide), lambda b, p, pt: (layer, pt[b, p], 0, 0)),
                pl.BlockSpec((None, None, PAGE_SIZE, wide), lambda b, p, pt: (layer, pt[b, p], 0, 0)),
                pl.BlockSpec((nrow, PAGE_SIZE), lambda b, p, pt: (0, p)),
                pl.BlockSpec((nrow, 1), lambda b, p, pt: (0, 0)),
                pl.BlockSpec((1, w), lambda b, p, pt: (0, 0)),
            ],
            out_specs=pl.BlockSpec((None, 1, wide), lambda b, p, pt: (b, 0, 0)),
            scratch_shapes=[pltpu.VMEM((1, nrow, 1), f32), pltpu.VMEM((1, nrow, 1), f32),
                            pltpu.VMEM((1, nrow, wide), f32), pltpu.VMEM((nrow, wide), jnp.bfloat16)],
        ),
        out_shape=jax.ShapeDtypeStruct((batch, 1, wide), f32),
        compiler_params=pltpu.CompilerParams(dimension_semantics=("parallel", "arbitrary")),
    )(page_table, lam.reshape(1, 1), vec(q), vec(k_new), vec(v_new), ck, cv, bias_past, bias0,
      g_norm.reshape(1, w))


def _rms_norm(x, g):
    xf = x.astype(jnp.float32)
    y = xf * lax.rsqrt(jnp.mean(xf * xf, axis=-1, keepdims=True) + EPS)
    return (y * g.astype(jnp.float32)).astype(x.dtype)


def _l2norm(x):
    return x * lax.rsqrt(jnp.sum(x * x, axis=-1, keepdims=True) + EPS)


def _complex_affine_combine(e1, e2):
    a1r, a1i, b1r, b1i = e1
    a2r, a2i, b2r, b2i = e2
    return (a2r * a1r - a2i * a1i, a2r * a1i + a2i * a1r,
            a2r * b1r - a2i * b1i + b2r, a2r * b1i + a2i * b1r + b2i)


def _s5(u, lam_re, lam_im, log_dt, b_re, b_im, c_re, c_im, d_skip, h0_re, h0_im):
    f32 = jnp.float32
    B, T, _ = u.shape
    uf = u.astype(f32).reshape(B, T, G_A, SSM_CH)
    lr, li = lam_re.astype(f32), lam_im.astype(f32)
    dt = jnp.exp(log_dt.astype(f32))[:, None]
    mag = jnp.exp(lr * dt)
    ab_re, ab_im = mag * jnp.cos(li * dt), mag * jnp.sin(li * dt)
    den = lr * lr + li * li
    fr = ((ab_re - 1.0) * lr + ab_im * li) / den
    fi = (ab_im * lr - (ab_re - 1.0) * li) / den
    br, bi = b_re.astype(f32), b_im.astype(f32)
    bb_re = fr[..., None] * br - fi[..., None] * bi
    bb_im = fr[..., None] * bi + fi[..., None] * br
    bu_re = jnp.einsum('btgc,gpc->btgp', uf, bb_re)
    bu_im = jnp.einsum('btgc,gpc->btgp', uf, bb_im)
    a_re = jnp.broadcast_to(ab_re, bu_re.shape)
    a_im = jnp.broadcast_to(ab_im, bu_im.shape)
    acc_re, acc_im, s_re, s_im = lax.associative_scan(
        _complex_affine_combine, (a_re, a_im, bu_re, bu_im), axis=1)
    h0r = h0_re.astype(f32)[:, None]
    h0i = h0_im.astype(f32)[:, None]
    hr = s_re + acc_re * h0r - acc_im * h0i
    hi = s_im + acc_re * h0i + acc_im * h0r
    y = (jnp.einsum('gcp,btgp->btgc', c_re.astype(f32), hr)
         - jnp.einsum('gcp,btgp->btgc', c_im.astype(f32), hi)
         + d_skip.astype(f32) * uf).reshape(B, T, W_A)
    return jax.nn.gelu(y), hr[:, -1], hi[:, -1]


def _to_chunks(a, L, n):
    B, T = a.shape[:2]
    a = jnp.pad(a, [(0, 0), (0, n * L - T)] + [(0, 0)] * (a.ndim - 2))
    a = a.reshape((B, n, L) + a.shape[2:])
    a = jnp.moveaxis(a, 1, 0)
    return jnp.swapaxes(a, 2, 3)


def _from_chunks(o, T):
    n, B, H, L = o.shape[:4]
    o = jnp.moveaxis(jnp.swapaxes(o, 2, 3), 0, 1)
    return o.reshape((B, n * L) + o.shape[3:])[:, :T]


def _gla_chunked(q, k, v, logf, S0):
    T = q.shape[1]
    L = min(CHUNK, T)
    n = -(-T // L)
    qc, kc, vc, gc = (_to_chunks(a, L, n) for a in (q, k, v, logf))
    tri = jnp.tril(jnp.ones((L, L), dtype=bool))

    def step(S, inp):
        qi, ki, vi, gi = inp
        b = jnp.cumsum(gi, axis=2)
        dec = jnp.exp(jnp.where(tri[:, :, None], b[:, :, :, None, :] - b[:, :, None, :, :], -jnp.inf))
        att = jnp.sum(qi[:, :, :, None, :] * ki[:, :, None, :, :] * dec, axis=-1)
        o = (jnp.einsum('bhtk,bhkv->bhtv', qi * jnp.exp(b), S)
             + jnp.einsum('bhts,bhsv->bhtv', att, vi))
        b_last = b[:, :, -1:, :]
        S_new = (jnp.exp(b_last[:, :, 0, :])[..., None] * S
                 + jnp.einsum('bhlk,bhlv->bhkv', ki * jnp.exp(b_last - b), vi))
        return S_new, o

    S, o = lax.scan(step, S0, (qc, kc, vc, gc))
    return _from_chunks(o, T), S


def _gdn_chunked(q, k, v, beta, logg, S0):
    T = q.shape[1]
    V = v.shape[-1]
    L = min(CHUNK, T)
    n = -(-T // L)
    qc, kc, vc, bc, gc = (_to_chunks(a, L, n) for a in (q, k, v, beta, logg))
    tri = jnp.tril(jnp.ones((L, L), dtype=bool))
    strict = jnp.tril(jnp.ones((L, L), dtype=bool), -1)
    eye = jnp.eye(L, dtype=jnp.float32)

    def step(S, inp):
        qi, ki, vi, bi, gi = inp
        gcum = jnp.cumsum(gi, axis=-1)
        ldec = jnp.exp(jnp.where(tri, gcum[..., :, None] - gcum[..., None, :], -jnp.inf))
        kb = ki * bi[..., None]
        m = jnp.where(strict, jnp.einsum('bhtk,bhsk->bhts', kb, ki) * ldec, 0.0)
        rhs = jnp.concatenate([vi * bi[..., None], kb * jnp.exp(gcum)[..., None]], axis=-1)
        sol = lax.linalg.triangular_solve(eye + m, rhs, left_side=True, lower=True, unit_diagonal=True)
        u, w = sol[..., :V], sol[..., V:]
        v_new = u - jnp.einsum('bhlk,bhkv->bhlv', w, S)
        att = jnp.einsum('bhtk,bhsk->bhts', qi, ki) * ldec
        o = (jnp.einsum('bhlk,bhkv->bhlv', qi * jnp.exp(gcum)[..., None], S)
             + jnp.einsum('bhts,bhsv->bhtv', att, v_new))
        g_last = gcum[..., -1]
        S_new = (jnp.exp(g_last)[..., None, None] * S
                 + jnp.einsum('bhlk,bhlv->bhkv', ki * jnp.exp(g_last[..., None] - gcum)[..., None], v_new))
        return S_new, o

    S, o = lax.scan(step, S0, (qc, kc, vc, bc, gc))
    return _from_chunks(o, T), S


def _causal_conv(x, buf, w):
    T = x.shape[1]
    xx = jnp.concatenate([buf.astype(x.dtype), x], axis=1)
    y = xx[:, 0:T] * w[0]
    for j in range(1, CONV_W):
        y = y + xx[:, j:j + T] * w[j]
    return jax.nn.silu(y), xx[:, T:]


def _t5_bucket(qpos, kpos):
    n = jnp.maximum(qpos[:, None] - kpos[None, :], 0)
    max_exact = N_BUCKETS // 2
    nf = jnp.maximum(n, 1).astype(jnp.float32)
    large = max_exact + (jnp.log(nf / max_exact) / math.log(MAX_DISTANCE / max_exact)
                         * (N_BUCKETS - max_exact)).astype(jnp.int32)
    large = jnp.minimum(large, N_BUCKETS - 1)
    return jnp.where(n < max_exact, n, large)


def _diff_attn_core(q, k, v, qpos, kpos, lam, rel_table):
    s = jnp.einsum('bqhcd,bkhcd->bchqk', q, k) * (DH_C ** -0.5)
    bias = jnp.transpose(rel_table.astype(jnp.float32)[_t5_bucket(qpos, kpos)], (2, 0, 1))
    causal = kpos[None, :] <= qpos[:, None]
    s = jnp.where(causal, s + bias, -jnp.inf)
    p = jax.nn.softmax(s, axis=-1)
    a = p[:, 0] - lam * p[:, 1]
    return jnp.einsum('bhqk,bkhe->bqhe', a, v)


def _diff_attn_prompt(q, k, v, lam, rel_table):
    B, T = q.shape[:2]
    qb = min(Q_BLOCK, T)
    nb = T // qb
    q_blocks = jnp.moveaxis(q.reshape((B, nb, qb) + q.shape[2:]), 1, 0)
    qpos = jnp.arange(T, dtype=jnp.int32).reshape(nb, qb)
    kpos = jnp.arange(T, dtype=jnp.int32)
    out = lax.map(lambda blk: _diff_attn_core(blk[0], k, v, blk[1], kpos, lam, rel_table), (q_blocks, qpos))
    return jnp.moveaxis(out, 0, 1).reshape((B, T) + out.shape[3:])


def _split_main():
    sizes = [W_A, H_B * DK_B, H_B * DK_B, H_B * DV_B, H_B * DV_B,
             2 * H_C * DH_C, 2 * H_C * DH_C, 2 * H_C * DH_C, W_D_QKV, H_D * DV_D]
    pts, acc = [], 0
    for s in sizes[:-1]:
        acc += s
        pts.append(acc)
    return pts


def _layer(l, x, mod, lb, p, past):
    f32 = jnp.float32
    bf16 = jnp.bfloat16
    B, T, _ = x.shape
    M = B * T
    sh1, sc1, gt1, sh2, sc2, gt2 = jnp.split(mod[:, None, :], 6, axis=-1)
    Mp = max(M, 16)
    tm, tn, tk = _tiles(Mp)

    def pad_rows(a):
        return a if Mp == M else jnp.pad(a, ((0, Mp - M), (0, 0)))

    h = _rms_norm(x, p['g_norm1'][l]) * (1.0 + sc1) + sh1
    h2d = pad_rows(h.reshape(M, D_MODEL).astype(bf16))
    z = _matmul(h2d, p['w_in'], l, N_IN_MAIN, tm=tm, tn=tn, tk=tk, out_dtype=f32)
    zt = _matmul(h2d, p['w_in_tail'], l, 128, tm=tm, tn=128, tk=tk, out_dtype=f32)
    z2d = z[:M]
    z = z2d.reshape(B, T, N_IN_MAIN)
    zt = zt[:M].reshape(B, T, 128)
    (a_u, b_q, b_f, b_i, b_g, c_q, c_k, c_v, d_qkv, d_g) = jnp.split(z, _split_main(), axis=-1)
    d_beta, d_a = zt[..., :H_D], zt[..., H_D:2 * H_D]

    if past is None:
        h0_re = jnp.zeros((B, G_A, P_A), f32)
        h0_im = jnp.zeros((B, G_A, P_A), f32)
        s0_b = jnp.zeros((B, H_B, DK_B, DV_B), f32)
        s0_d = jnp.zeros((B, H_D, DK_D, DV_D), f32)
        conv0 = jnp.zeros((B, CONV_W - 1, W_D_QKV), z.dtype)
    else:
        h0_re, h0_im = past['ssm_re'][l], past['ssm_im'][l]
        s0_b, s0_d, conv0 = past['hgrn'][l], past['gdn'][l], past['conv'][l]

    s5_params = (p['ssm_lam_re'][l], p['ssm_lam_im'][l], p['ssm_log_dt'][l], p['ssm_b_re'][l],
                 p['ssm_b_im'][l], p['ssm_c_re'][l], p['ssm_c_im'][l], p['ssm_d'][l])
    if past is None:
        y_a, hr, hi = _s5_prompt(a_u, _s5_tables(*s5_params, T // S5_L))
    else:
        y_a, hr, hi = _s5_step(a_u, h0_re, h0_im, _s5_tables(*s5_params, 1, L=1))
    o_a = _s5_glu(pad_rows(y_a.reshape(M, W_A)), p['ssm_w_glu'], p['ssm_b_glu'], l,
                  min(Mp, 512))[:M].reshape(B, T, W_A)

    qb_ = jax.nn.silu(b_q.astype(f32)).reshape(B, T, H_B, DK_B) * (DK_B ** -0.5)
    fl = b_f.astype(f32).reshape(B, T, H_B, DK_B)
    lbh = lb.reshape(H_B, DK_B)
    logf = jnp.logaddexp(jnp.log(lbh), jnp.log1p(-lbh) + jax.nn.log_sigmoid(fl))
    kb_ = (1.0 - lbh) * jax.nn.sigmoid(-fl)
    vb_ = b_i.astype(f32).reshape(B, T, H_B, DV_B)
    o_b, s_b = _gla_chunked(qb_, kb_, vb_, logf, s0_b.astype(f32))
    o_b = (_rms_norm(o_b, p['hgrn_norm'][l]) * jax.nn.silu(b_g.astype(f32).reshape(B, T, H_B, DV_B))).reshape(B, T, H_B * DV_B)

    lam_init = 0.8 - 0.6 * math.exp(-0.3 * l)
    lam = (jnp.exp(jnp.sum(p['diff_lq1'][l].astype(f32) * p['diff_lk1'][l].astype(f32)))
           - jnp.exp(jnp.sum(p['diff_lq2'][l].astype(f32) * p['diff_lk2'][l].astype(f32))) + lam_init)
    kh = c_k.reshape(B, T, H_C, 2 * DH_C)
    vh = c_v.reshape(B, T, H_C, 2 * DH_C)
    if past is None:
        o_c = _diff_attn_prompt_pallas(z2d, B, T, lam, _attn_bias_tiles(p['rel_bias']),
                                       p['diff_norm'][l], 1.0 - lam_init)
    else:
        o_c = _diff_attn_step_pallas(c_q.reshape(M, -1), c_k.reshape(M, -1), c_v.reshape(M, -1),
                                     past['cache_k'], past['cache_v'], past['page_table'], l, lam,
                                     p['rel_bias'], p['diff_norm'][l], 1.0 - lam_init)
    o_c = o_c.reshape(B, T, 2 * H_C * DH_C)

    qkv, conv_new = _causal_conv(d_qkv, conv0, p['gdn_conv'][l])
    qkv = qkv.astype(f32)
    dq, dk, dv = jnp.split(qkv, [H_D * DK_D, 2 * H_D * DK_D], axis=-1)
    dq = _l2norm(dq.reshape(B, T, H_D, DK_D)) * (DK_D ** -0.5)
    dk = _l2norm(dk.reshape(B, T, H_D, DK_D))
    dv = dv.reshape(B, T, H_D, DV_D)
    beta = jax.nn.sigmoid(d_beta.astype(f32))
    logg = -jnp.exp(p['gdn_a_log'][l].astype(f32)) * jax.nn.softplus(d_a.astype(f32) + p['gdn_dt_bias'][l].astype(f32))
    o_d, s_d = _gdn_chunked(dq, dk, dv, beta, logg, s0_d.astype(f32))
    o_d = (_rms_norm(o_d, p['gdn_norm'][l]) * jax.nn.silu(d_g.astype(f32).reshape(B, T, H_D, DV_D))).reshape(B, T, H_D * DV_D)

    o = pad_rows(jnp.concatenate([o_a, o_b, o_c, o_d], axis=-1).reshape(M, D_MODEL).astype(bf16))
    x2d = pad_rows(x.reshape(M, D_MODEL))

    def gate_rows(g):
        if T < tm:
            return pad_rows(jnp.broadcast_to(g, (B, T, D_MODEL)).reshape(M, D_MODEL)), \
                pl.BlockSpec((tm, tn), lambda i, j, k: (i, j))
        per = T // tm
        return g.reshape(B, 1, D_MODEL), \
            pl.BlockSpec((None, 1, tn), lambda i, j, k: (i // per, 0, j))

    g1, g1_spec = gate_rows(gt1)
    x2d = _matmul(o, p['w_out'], l, D_MODEL, tm=tm, tn=tn, tk=tk, out_dtype=f32,
                  epilogue="gated_residual", extras=(x2d, g1),
                  extra_specs=(pl.BlockSpec((tm, tn), lambda i, j, k: (i, j)), g1_spec))

    xm = x2d[:M].reshape(B, T, D_MODEL)
    h2 = pad_rows((_rms_norm(xm, p['g_norm2'][l]) * (1.0 + sc2) + sh2).reshape(M, D_MODEL).astype(bf16))
    ff = _matmul(h2, p['w_up'], l, D_FF, tm=tm, tn=tn, tk=tk, out_dtype=bf16, epilogue="relu2")
    g2, g2_spec = gate_rows(gt2)
    x2d = _matmul(ff, p['w_down'], l, D_MODEL, tm=tm, tn=tn, tk=tk, out_dtype=f32,
                  epilogue="gated_residual", extras=(x2d, g2),
                  extra_specs=(pl.BlockSpec((tm, tn), lambda i, j, k: (i, j)), g2_spec))
    return x2d[:M].reshape(B, T, D_MODEL), (kh, vh, hr, hi, s_b, s_d, conv_new)


def _trunk(x, mods, p, past):
    sm = jax.nn.softmax(p['hgrn_lb_logits'].astype(jnp.float32), axis=0)
    lb_all = jnp.cumsum(sm, axis=0)
    lb_all = lb_all - lb_all[0:1]
    cols = [[] for _ in range(7)]
    for l in range(DEPTH):
        x, st = _layer(l, x, mods[l], lb_all[l], p, past)
        for j in range(7):
            cols[j].append(st[j])
    y = _rms_norm(x, p['g_final'])
    return y, [jnp.stack(col, axis=0) for col in cols]


def kernel(x_prompt, x_sample, c_prompt, c_sample, cache_k, cache_v, page_table, state_ssm_re, state_ssm_im, state_hgrn, state_gdn, state_gdn_conv, w_ada, b_ada, g_norm1, w_in, ssm_lam_re, ssm_lam_im, ssm_log_dt, ssm_b_re, ssm_b_im, ssm_c_re, ssm_c_im, ssm_d, ssm_w_glu, ssm_b_glu, hgrn_lb_logits, hgrn_norm, diff_lq1, diff_lk1, diff_lq2, diff_lk2, diff_norm, rel_bias, gdn_conv, gdn_a_log, gdn_dt_bias, gdn_norm, w_out, g_norm2, w_up, w_down, g_final):
    w_in_tail = jnp.pad(w_in[:, :, N_IN_MAIN:], ((0, 0), (0, 0), (0, 128 - N_TAIL)))
    p = dict(w_ada=w_ada, b_ada=b_ada, g_norm1=g_norm1, w_in=w_in, w_in_tail=w_in_tail,
             ssm_lam_re=ssm_lam_re, ssm_lam_im=ssm_lam_im, ssm_log_dt=ssm_log_dt,
             ssm_b_re=ssm_b_re, ssm_b_im=ssm_b_im, ssm_c_re=ssm_c_re, ssm_c_im=ssm_c_im,
             ssm_d=ssm_d, ssm_w_glu=ssm_w_glu, ssm_b_glu=ssm_b_glu,
             hgrn_lb_logits=hgrn_lb_logits, hgrn_norm=hgrn_norm,
             diff_lq1=diff_lq1, diff_lk1=diff_lk1, diff_lq2=diff_lq2, diff_lk2=diff_lk2,
             diff_norm=diff_norm, rel_bias=rel_bias,
             gdn_conv=gdn_conv, gdn_a_log=gdn_a_log, gdn_dt_bias=gdn_dt_bias, gdn_norm=gdn_norm,
             w_out=w_out, g_norm2=g_norm2, w_up=w_up, w_down=w_down, g_final=g_final)

    nb_p, nb_s = c_prompt.shape[0], c_sample.shape[0]
    c_all = jnp.concatenate([c_prompt, c_sample], axis=0)
    rows = 16
    cs = jnp.pad(jax.nn.silu(c_all), ((0, rows - nb_p - nb_s), (0, 0))).astype(jnp.bfloat16)
    mods_p, mods_s = [], []
    for l in range(DEPTH):
        m = _matmul(cs, w_ada, l, 6 * D_MODEL, tm=rows, tn=2048, tk=1024, out_dtype=jnp.float32,
                    epilogue="bias", extras=(b_ada[l][None, :],),
                    extra_specs=(pl.BlockSpec((1, 2048), lambda i, j, k: (0, j)),))
        mods_p.append(m[:nb_p])
        mods_s.append(m[nb_p:nb_p + nb_s])

    y_prompt, sp = _trunk(x_prompt, mods_p, p, None)

    past = dict(cache_k=cache_k, cache_v=cache_v, page_table=page_table,
                ssm_re=state_ssm_re, ssm_im=state_ssm_im, hgrn=state_hgrn,
                gdn=state_gdn, conv=state_gdn_conv)
    y_sample, ss = _trunk(x_sample, mods_s, p, past)
    return (y_prompt, y_sample,
            sp[0], sp[1], sp[2], sp[3], sp[4], sp[5], sp[6],
            ss[0], ss[1], ss[2], ss[3], ss[4], ss[5], ss[6])
```

```python
import functools
import math

import jax
import jax.numpy as jnp
from jax import lax
from jax.experimental import pallas as pl
from jax.experimental.pallas import tpu as pltpu

D_MODEL = 4096
DEPTH = 4
PAGE_SIZE = 128
GROUP_WIDTH = D_MODEL // 4
SSM_CH = 16
G_A = GROUP_WIDTH // SSM_CH
P_A = 64
W_A = G_A * SSM_CH
H_B = 8
DK_B = 128
DV_B = GROUP_WIDTH // H_B
H_C = 4
DH_C = GROUP_WIDTH // (2 * H_C)
N_BUCKETS = 32
MAX_DISTANCE = 128
Q_BLOCK = 128
H_D = 8
DK_D = 128
DV_D = GROUP_WIDTH // H_D
CONV_W = 4
W_D_QKV = H_D * (2 * DK_D + DV_D)
CHUNK = 64
D_FF = 4 * D_MODEL
N_IN = W_A + 2 * H_B * DK_B + 2 * H_B * DV_B + 6 * H_C * DH_C + W_D_QKV + H_D * DV_D + 2 * H_D
N_IN_MAIN = (N_IN // 128) * 128
N_TAIL = N_IN - N_IN_MAIN
EPS = 1e-6

VMEM_LIMIT_BYTES = 56 * 1024 * 1024


def _mm_epilogue(acc, extra_refs, epilogue):
    if epilogue == "none":
        return acc
    if epilogue == "bias":
        return acc + extra_refs[0][...]
    if epilogue == "relu2":
        r = jnp.maximum(acc, 0.0)
        return r * r
    if epilogue == "gated_residual":
        return extra_refs[0][...] + extra_refs[1][...] * acc
    raise ValueError(epilogue)


def _mm_body(a_ref, w_ref, *rest, nk, epilogue):
    prod = jnp.dot(a_ref[...], w_ref[...].astype(jnp.bfloat16), preferred_element_type=jnp.float32)
    if nk == 1:
        *extra_refs, o_ref = rest
        o_ref[...] = _mm_epilogue(prod, extra_refs, epilogue).astype(o_ref.dtype)
        return
    *extra_refs, o_ref, acc_ref = rest
    k = pl.program_id(2)

    @pl.when(k == 0)
    def _():
        acc_ref[...] = prod

    @pl.when((k > 0) & (k < nk - 1))
    def _():
        acc_ref[...] += prod

    @pl.when(k == nk - 1)
    def _():
        o_ref[...] = _mm_epilogue(acc_ref[...] + prod, extra_refs, epilogue).astype(o_ref.dtype)


def _matmul(a, w, layer, n_cols, *, tm, tn, tk, out_dtype, epilogue="none",
            extras=(), extra_specs=()):
    m, k_dim = a.shape
    assert w.shape[1] == k_dim and m % tm == 0 and n_cols % tn == 0 and k_dim % tk == 0
    nk = k_dim // tk
    grid = (m // tm, n_cols // tn, nk)
    in_specs = [
        pl.BlockSpec((tm, tk), lambda i, j, k: (i, k)),
        pl.BlockSpec((None, tk, tn), lambda i, j, k: (layer, k, j)),
    ] + list(extra_specs)
    return pl.pallas_call(
        functools.partial(_mm_body, nk=nk, epilogue=epilogue),
        grid=grid,
        in_specs=in_specs,
        out_specs=pl.BlockSpec((tm, tn), lambda i, j, k: (i, j)),
        out_shape=jax.ShapeDtypeStruct((m, n_cols), out_dtype),
        scratch_shapes=[pltpu.VMEM((tm, tn), jnp.float32)] if nk > 1 else [],
        compiler_params=pltpu.CompilerParams(
            dimension_semantics=("parallel", "parallel", "arbitrary"),
            vmem_limit_bytes=VMEM_LIMIT_BYTES),
    )(a, w, *extras)


def _tiles(m):
    if m >= 1024:
        return 1024, 512, 4096
    return m, 2048, 1024


def _split_bf16(x):
    hi = x.astype(jnp.bfloat16)
    lo = (x - hi.astype(jnp.float32)).astype(jnp.bfloat16)
    return hi, lo


def _dot3(a, b, dims=(((1,), (0,)), ((), ()))):
    ah, al = _split_bf16(a)
    bh, bl = _split_bf16(b)

    def d(x, y):
        return lax.dot_general(x, y, dims, preferred_element_type=jnp.float32)

    return d(ah, bh) + (d(ah, bl) + d(al, bh))


_NT = (((1,), (1,)), ((), ()))
_TN = (((0,), (0,)), ((), ()))


S5_L = 16


def _s5_tables(lam_re, lam_im, log_dt, b_re, b_im, c_re, c_im, d_skip, n_chunks, L=S5_L):
    f32 = jnp.float32
    hp = lax.Precision.HIGHEST
    G, P, C = G_A, P_A, SSM_CH
    lr, li = lam_re.astype(f32), lam_im.astype(f32)
    dt = jnp.exp(log_dt.astype(f32))[:, None]
    mag = jnp.exp(lr * dt)
    ab_re, ab_im = mag * jnp.cos(li * dt), mag * jnp.sin(li * dt)
    den = lr * lr + li * li
    fr = ((ab_re - 1.0) * lr + ab_im * li) / den
    fi = (ab_im * lr - (ab_re - 1.0) * li) / den
    br, bi = b_re.astype(f32), b_im.astype(f32)
    bb_re = fr[..., None] * br - fi[..., None] * bi
    bb_im = fr[..., None] * bi + fi[..., None] * br

    def powers(ns):
        nf = jnp.asarray(ns, f32)[None, :, None]
        m = jnp.exp(nf * (lr * dt)[:, None, :])
        ph = nf * (li * dt)[:, None, :]
        return m * jnp.cos(ph), m * jnp.sin(ph)

    ar, ai = powers(list(range(L + 1)))
    cr, ci = c_re.astype(f32), c_im.astype(f32)
    car = cr[:, None] * ar[:, :, None, :] - ci[:, None] * ai[:, :, None, :]
    cai = cr[:, None] * ai[:, :, None, :] + ci[:, None] * ar[:, :, None, :]
    kk = (jnp.einsum('gncp,gpd->gncd', car[:, :L], bb_re, precision=hp)
          - jnp.einsum('gncp,gpd->gncd', cai[:, :L], bb_im, precision=hp))
    tl = jnp.arange(L)
    lag = tl[:, None] - tl[None, :]
    kfull = kk[:, jnp.clip(lag, 0, L - 1)]
    kfull = jnp.where((lag >= 0)[None, :, :, None, None], kfull, 0.0)
    wt = jnp.transpose(kfull, (0, 2, 4, 1, 3)).reshape(G, L * C, L * C)
    arr, aii = ar[:, L - 1 - tl], ai[:, L - 1 - tl]
    bm_re = arr[:, :, None, :] * jnp.swapaxes(bb_re, 1, 2)[:, None] - aii[:, :, None, :] * jnp.swapaxes(bb_im, 1, 2)[:, None]
    bm_im = arr[:, :, None, :] * jnp.swapaxes(bb_im, 1, 2)[:, None] + aii[:, :, None, :] * jnp.swapaxes(bb_re, 1, 2)[:, None]
    bmt = jnp.concatenate([bm_re, bm_im], axis=-1).reshape(G, L * C, 2 * P)
    cm_re = jnp.transpose(car[:, 1:], (0, 3, 1, 2))
    cm_im = -jnp.transpose(cai[:, 1:], (0, 3, 1, 2))
    cmt = jnp.concatenate([cm_re, cm_im], axis=1).reshape(G, 2 * P, L * C)
    n_steps = max(1, (n_chunks - 1).bit_length())
    sr, si = powers([L * (1 << i) for i in range(n_steps)])
    ca = jnp.concatenate([sr, sr], axis=-1)
    cb = jnp.concatenate([-si, si], axis=-1)
    dtile = jnp.tile(d_skip.astype(f32), (1, L))[:, None, :]
    return wt, bmt, cmt, ca, cb, dtile, n_steps


def _s5_body(x_ref, wt_ref, bm_ref, cm_ref, ca_ref, cb_ref, d_ref, y_ref, st_ref, *,
             n_chunks, n_steps, batch):
    x = x_ref[...]
    rows = x.shape[0]
    e = _dot3(x, bm_ref[...])
    jidx = lax.broadcasted_iota(jnp.int32, (rows, 2 * P_A), 0) % n_chunks
    for i in range(n_steps):
        sh = 1 << i
        xs = jnp.where(jidx >= sh, pltpu.roll(e, sh, axis=0), 0.0)
        e = e + xs * ca_ref[i:i + 1, :] + pltpu.roll(xs, P_A, axis=1) * cb_ref[i:i + 1, :]
    hprev = jnp.where(jidx >= 1, pltpu.roll(e, 1, axis=0), 0.0)
    y_ref[...] = _dot3(x, wt_ref[...]) + _dot3(hprev, cm_ref[...]) + x * d_ref[...]
    for b in range(batch):
        r = b * n_chunks + n_chunks - 1
        st_ref[b:b + 1, :] = e[r:r + 1, :]


def _s5_prompt(a_u, tables):
    wt, bmt, cmt, ca, cb, dtile, n_steps = tables
    B, T, _ = a_u.shape
    L, G, C, P = S5_L, G_A, SSM_CH, P_A
    nc = T // L
    rows = B * nc
    x = a_u.reshape(B, nc, L, G, C).transpose(3, 0, 1, 2, 4).reshape(G, rows, L * C)
    full = lambda *shape: pl.BlockSpec((None,) + shape, lambda g: (g,) + (0,) * len(shape))
    y, st = pl.pallas_call(
        functools.partial(_s5_body, n_chunks=nc, n_steps=n_steps, batch=B),
        grid=(G,),
        in_specs=[full(rows, L * C), full(L * C, L * C), full(L * C, 2 * P), full(2 * P, L * C),
                  full(n_steps, 2 * P), full(n_steps, 2 * P), full(1, L * C)],
        out_specs=[full(rows, L * C), full(B, 2 * P)],
        out_shape=[jax.ShapeDtypeStruct((G, rows, L * C), jnp.float32),
                   jax.ShapeDtypeStruct((G, B, 2 * P), jnp.float32)],
        compiler_params=pltpu.CompilerParams(dimension_semantics=("parallel",)),
    )(x, wt, bmt, cmt, ca, cb, dtile)
    y = y.reshape(G, B, nc, L, C).transpose(1, 2, 3, 0, 4).reshape(B, T, G * C)
    st = jnp.swapaxes(st, 0, 1)
    return y, st[..., :P], st[..., P:]


def _s5_step_body(x_ref, h_ref, w_ref, bm_ref, cm_ref, ca_ref, cb_ref, d_ref, y_ref, st_ref):
    x = x_ref[...]
    h0 = h_ref[...]
    st_ref[...] = (_dot3(x, bm_ref[...]) + h0 * ca_ref[...]
                   + pltpu.roll(h0, P_A, axis=1) * cb_ref[...])
    y_ref[...] = _dot3(x, w_ref[...]) + _dot3(h0, cm_ref[...]) + x * d_ref[...]


def _s5_step(a_u, h0_re, h0_im, tables):
    wt, bmt, cmt, ca, cb, dtile, _ = tables
    B = a_u.shape[0]
    G, C, P = G_A, SSM_CH, P_A
    x = a_u.reshape(B, G, C).transpose(1, 0, 2)
    h0 = jnp.concatenate([h0_re, h0_im], axis=-1).transpose(1, 0, 2)
    full = lambda *shape: pl.BlockSpec((None,) + shape, lambda g: (g,) + (0,) * len(shape))
    y, st = pl.pallas_call(
        _s5_step_body,
        grid=(G,),
        in_specs=[full(B, C), full(B, 2 * P), full(C, C), full(C, 2 * P), full(2 * P, C),
                  full(1, 2 * P), full(1, 2 * P), full(1, C)],
        out_specs=[full(B, C), full(B, 2 * P)],
        out_shape=[jax.ShapeDtypeStruct((G, B, C), jnp.float32),
                   jax.ShapeDtypeStruct((G, B, 2 * P), jnp.float32)],
        compiler_params=pltpu.CompilerParams(dimension_semantics=("parallel",)),
    )(x, h0.astype(jnp.float32), wt, bmt, cmt, ca, cb, dtile)
    y = y.transpose(1, 0, 2).reshape(B, 1, G * C)
    st = jnp.swapaxes(st, 0, 1)
    return y, st[..., :P], st[..., P:]


def _gelu_tanh(x):
    c = math.sqrt(2.0 / math.pi)
    return 0.5 * x * (1.0 + jnp.tanh(c * (x + 0.044715 * (x * x * x))))


def _glu_body(y_ref, w_ref, b_ref, o_ref):
    zg = _gelu_tanh(y_ref[...])
    glu = jnp.dot(zg.astype(jnp.bfloat16), w_ref[...].astype(jnp.bfloat16),
                  preferred_element_type=jnp.float32) + b_ref[...]
    o_ref[...] = (zg * jax.nn.sigmoid(glu)).astype(o_ref.dtype)


def _s5_glu(y2d, w_glu, b_glu, layer, tm):
    m = y2d.shape[0]
    return pl.pallas_call(
        _glu_body,
        grid=(m // tm,),
        in_specs=[pl.BlockSpec((tm, W_A), lambda i: (i, 0)),
                  pl.BlockSpec((None, W_A, W_A), lambda i: (layer, 0, 0)),
                  pl.BlockSpec((None, 1, W_A), lambda i: (layer, 0, 0))],
        out_specs=pl.BlockSpec((tm, W_A), lambda i: (i, 0)),
        out_shape=jax.ShapeDtypeStruct((m, W_A), jnp.bfloat16),
        compiler_params=pltpu.CompilerParams(dimension_semantics=("parallel",),
                                             vmem_limit_bytes=VMEM_LIMIT_BYTES),
    )(y2d, w_glu, b_glu.reshape(DEPTH, 1, W_A))


ATT_T = 256
MASKED = -1e30
_CQ_BLK = (W_A + 2 * H_B * DK_B + 2 * H_B * DV_B) // (2 * DH_C)
_CK_BLK = _CQ_BLK + H_C
_CV_BLK = _CK_BLK + H_C


def _attn_bias_tiles(rel_bias):
    t = ATT_T
    assert t >= MAX_DISTANCE
    qpos = jnp.arange(3 * t, dtype=jnp.int32)
    kpos = jnp.arange(t, dtype=jnp.int32)
    bias = rel_bias.astype(jnp.float32)[_t5_bucket(qpos, kpos)]
    bias = jnp.where((kpos[None, :] <= qpos[:, None])[..., None], bias, MASKED)
    return bias.transpose(2, 0, 1).reshape(H_C, 3, t, t)


def _softmax_update(c, s, v_bf16, m_ref, l_ref, acc_ref):
    m_prev = m_ref[c]
    m_new = jnp.maximum(m_prev, jnp.max(s, axis=-1, keepdims=True))
    alpha = jnp.exp(m_prev - m_new)
    p = jnp.exp(s - m_new)
    l_ref[c] = alpha * l_ref[c] + jnp.sum(p, axis=-1, keepdims=True)
    acc_ref[c] = alpha * acc_ref[c] + jnp.dot(p.astype(jnp.bfloat16), v_bf16,
                                              preferred_element_type=jnp.float32)
    m_ref[c] = m_new


def _head_norm(o, g, out_scale):
    return o * lax.rsqrt(jnp.mean(o * o, axis=-1, keepdims=True) + EPS) * g * out_scale


def _attn_body(lam_ref, q_ref, k_ref, v_ref, bias_ref, g_ref, o_ref, m_ref, l_ref, acc_ref, *,
               out_scale):
    qi = pl.program_id(2)
    ki = pl.program_id(3)

    @pl.when(ki == 0)
    def _():
        m_ref[...] = jnp.full_like(m_ref, MASKED)
        l_ref[...] = jnp.zeros_like(l_ref)
        acc_ref[...] = jnp.zeros_like(acc_ref)

    @pl.when(ki <= qi)
    def _():
        q = q_ref[...].astype(jnp.bfloat16)
        k = k_ref[...].astype(jnp.bfloat16)
        v = v_ref[...].astype(jnp.bfloat16)
        bias = bias_ref[...]
        for c in range(2):
            s = lax.dot_general(q[:, c * DH_C:(c + 1) * DH_C], k[:, c * DH_C:(c + 1) * DH_C], _NT,
                                preferred_element_type=jnp.float32) * (DH_C ** -0.5) + bias
            _softmax_update(c, s, v, m_ref, l_ref, acc_ref)

    @pl.when(ki == qi)
    def _():
        o = acc_ref[0] / l_ref[0] - lam_ref[...] * (acc_ref[1] / l_ref[1])
        o_ref[...] = _head_norm(o, g_ref[...], out_scale).astype(o_ref.dtype)


def _diff_attn_prompt_pallas(z2d, batch, seq, lam, bias_tiles, g_norm, out_scale):
    t = ATT_T
    nq = seq // t
    w = 2 * DH_C
    grid = (batch, H_C, nq, nq)
    return pl.pallas_call(
        functools.partial(_attn_body, out_scale=out_scale),
        grid=grid,
        in_specs=[
            pl.BlockSpec((1, 1), lambda b, h, qi, ki: (0, 0)),
            pl.BlockSpec((t, w), lambda b, h, qi, ki: (b * nq + qi, _CQ_BLK + h)),
            pl.BlockSpec((t, w), lambda b, h, qi, ki: (b * nq + jnp.minimum(ki, qi), _CK_BLK + h)),
            pl.BlockSpec((t, w), lambda b, h, qi, ki: (b * nq + jnp.minimum(ki, qi), _CV_BLK + h)),
            pl.BlockSpec((None, None, t, t),
                         lambda b, h, qi, ki: (h, jnp.clip(qi - ki, 0, 2), 0, 0)),
            pl.BlockSpec((1, w), lambda b, h, qi, ki: (0, 0)),
        ],
        out_specs=pl.BlockSpec((t, w), lambda b, h, qi, ki: (b * nq + qi, h)),
        out_shape=jax.ShapeDtypeStruct((batch * seq, H_C * w), jnp.bfloat16),
        scratch_shapes=[pltpu.VMEM((2, t, 1), jnp.float32), pltpu.VMEM((2, t, 1), jnp.float32),
                        pltpu.VMEM((2, t, w), jnp.float32)],
        compiler_params=pltpu.CompilerParams(
            dimension_semantics=("parallel", "parallel", "parallel", "arbitrary")),
    )(lam.reshape(1, 1), z2d, z2d, z2d, bias_tiles, g_norm.reshape(1, w))


def _attn_step_body(pt_ref, lam_ref, q_ref, kn_ref, vn_ref, kc_ref, vc_ref, bias_ref, bias0_ref, g_ref,
                    o_ref, m_ref, l_ref, acc_ref, qrows_ref, *, n_pages, out_scale):
    del pt_ref
    p = pl.program_id(1)
    w = 2 * DH_C
    nrow = qrows_ref.shape[1]
    scale = DH_C ** -0.5

    @pl.when(p == 0)
    def _():
        col = lax.broadcasted_iota(jnp.int32, (nrow, w), 1)
        row = lax.broadcasted_iota(jnp.int32, (nrow, w), 0)
        for h in range(H_C):
            hs = slice(h * w, (h + 1) * w)
            qrows = jnp.where(col // DH_C == row, q_ref[:, hs], 0.0)
            qrows_ref[h] = qrows.astype(jnp.bfloat16)
            s_new = jnp.sum(qrows * kn_ref[:, hs], axis=-1, keepdims=True) * scale + bias0_ref[h]
            m_ref[h] = s_new
            l_ref[h] = jnp.ones_like(s_new)
            acc_ref[h] = jnp.broadcast_to(vn_ref[:, hs], (nrow, w))

    for h in range(H_C):
        k = kc_ref[:, h, :].astype(jnp.bfloat16)
        v = vc_ref[:, h, :].astype(jnp.bfloat16)
        s = lax.dot_general(qrows_ref[h], k, _NT, preferred_element_type=jnp.float32) * scale + bias_ref[h]
        _softmax_update(h, s, v, m_ref, l_ref, acc_ref)

    @pl.when(p == n_pages - 1)
    def _():
        for h in range(H_C):
            a = acc_ref[h] / l_ref[h]
            o = a[0:1, :] - lam_ref[...] * a[1:2, :]
            o_ref[:, h * w:(h + 1) * w] = _head_norm(o, g_ref[...], out_scale)


def _diff_attn_step_pallas(q, k_new, v_new, cache_k, cache_v, page_table, layer, lam, rel_bias,
                           g_norm, out_scale):
    f32 = jnp.float32
    batch, n_pages = page_table.shape
    w = 2 * DH_C
    wide = H_C * w
    nrow = 16
    past_len = n_pages * PAGE_SIZE
    bucket = _t5_bucket(jnp.full((1,), past_len, jnp.int32), jnp.arange(past_len + 1, dtype=jnp.int32))[0]
    bias_h = rel_bias.astype(f32)[bucket].T.reshape(H_C, 1, past_len + 1)
    bias_past, bias0 = bias_h[:, :, :past_len], bias_h[:, :, past_len:]
    vec = lambda a: a.reshape(batch, 1, wide).astype(f32)
    row_spec = pl.BlockSpec((None, 1, wide), lambda b, p, pt: (b, 0, 0))
    page_spec = pl.BlockSpec((None, None, PAGE_SIZE, H_C, w), lambda b, p, pt: (layer, pt[b, p], 0, 0, 0))
    return pl.pallas_call(
        functools.partial(_attn_step_body, n_pages=n_pages, out_scale=out_scale),
        grid_spec=pltpu.PrefetchScalarGridSpec(
            num_scalar_prefetch=1,
            grid=(batch, n_pages),
            in_specs=[
                pl.BlockSpec((1, 1), lambda b, p, pt: (0, 0)),
                row_spec, row_spec, row_spec,
                page_spec, page_spec,
                pl.BlockSpec((H_C, 1, PAGE_SIZE), lambda b, p, pt: (0, 0, p)),
                pl.BlockSpec((H_C, 1, 1), lambda b, p, pt: (0, 0, 0)),
                pl.BlockSpec((1, w), lambda b, p, pt: (0, 0)),
            ],
            out_specs=pl.BlockSpec((None, 1, wide), lambda b, p, pt: (b, 0, 0)),
            scratch_shapes=[pltpu.VMEM((H_C, nrow, 1), f32), pltpu.VMEM((H_C, nrow, 1), f32),
                            pltpu.VMEM((H_C, nrow, w), f32), pltpu.VMEM((H_C, nrow, w), jnp.bfloat16)],
        ),
        out_shape=jax.ShapeDtypeStruct((batch, 1, wide), f32),
        compiler_params=pltpu.CompilerParams(dimension_semantics=("parallel", "arbitrary")),
    )(page_table, lam.reshape(1, 1), vec(q), vec(k_new), vec(v_new), cache_k, cache_v, bias_past, bias0,
      g_norm.reshape(1, w))


_DQ_BLK = (N_IN_MAIN - H_D * DV_D - W_D_QKV) // GROUP_WIDTH
_DG_BLK = (N_IN_MAIN - H_D * DV_D) // GROUP_WIDTH
CONV_CARRY = 8


def _silu(x):
    return x * jax.nn.sigmoid(x)


def _softplus(x):
    return jnp.maximum(x, 0.0) + jnp.log(1.0 + jnp.exp(-jnp.abs(x)))


def _cumsum_rows(x):
    n = x.shape[0]
    row = lax.broadcasted_iota(jnp.int32, x.shape, 0)
    sh = 1
    while sh < n:
        x = x + jnp.where(row >= sh, pltpu.roll(x, sh, axis=0), 0.0)
        sh *= 2
    return x


def _unit_lower_inverse(m):
    n = m.shape[0]
    r = lax.broadcasted_iota(jnp.int32, (n, n), 0)
    c = lax.broadcasted_iota(jnp.int32, (n, n), 1)
    eye = (r == c).astype(jnp.float32)
    t = eye - jnp.where(r // 2 == c // 2, m, 0.0)
    sz = 2
    while sz < n:
        ms = jnp.where((r // (2 * sz) == c // (2 * sz)) & (r // sz != c // sz), m, 0.0)
        t = t - _dot3(_dot3(t, ms), t)
        sz *= 2
    return t


def _l2n(x):
    return x * lax.rsqrt(jnp.sum(x * x, axis=-1, keepdims=True) + EPS)


def _gdn_body(xq_ref, xk_ref, xv_ref, gate_ref, tail_ref, cw_ref, ab_ref, ng_ref,
              o_ref, sout_ref, s_ref, carry_ref, *, n_chunks):
    bf16 = jnp.bfloat16
    L = CHUNK
    ci = pl.program_id(1)

    @pl.when(ci == 0)
    def _():
        s_ref[...] = jnp.zeros_like(s_ref)
        carry_ref[...] = jnp.zeros_like(carry_ref)

    def conv(x_ref, part):
        x = x_ref[...]
        cols = slice(part * GROUP_WIDTH, (part + 1) * GROUP_WIDTH)
        xc = jnp.concatenate([carry_ref[:, cols], x], axis=0)
        w = cw_ref[:, cols]
        base = CONV_CARRY - (CONV_W - 1)
        y = xc[base:base + L] * w[0:1]
        for j in range(1, CONV_W):
            y = y + xc[base + j:base + j + L] * w[j:j + 1]
        carry_ref[:, cols] = x[L - CONV_CARRY:L]
        return _silu(y)

    qc, kc, vc = conv(xq_ref, 0), conv(xk_ref, 1), conv(xv_ref, 2)
    gate = gate_ref[...]
    tail = tail_ref[...]
    beta_all = jax.nn.sigmoid(tail)
    logg = -jnp.exp(ab_ref[0:1, :]) * _softplus(tail + ab_ref[1:2, :])
    gcum_all = _cumsum_rows(logg)
    gcum_t = jnp.concatenate([gcum_all, jnp.zeros((128 - L, 128), jnp.float32)], axis=0).T[:, :L]

    r = lax.broadcasted_iota(jnp.int32, (L, L), 0)
    c = lax.broadcasted_iota(jnp.int32, (L, L), 1)
    tri = r >= c
    strict = r > c

    for h in range(H_D):
        hs = slice(h * DK_D, (h + 1) * DK_D)
        q = _l2n(qc[:, hs]) * (DK_D ** -0.5)
        k = _l2n(kc[:, hs])
        v = vc[:, hs]
        beta = beta_all[:, h:h + 1]
        g_col = gcum_all[:, H_D + h:H_D + h + 1]
        g_row = gcum_t[H_D + h:H_D + h + 1, :]
        ldec = jnp.exp(jnp.minimum(g_col - g_row, 0.0))
        kb = k * beta
        k16 = k.astype(bf16)
        m = jnp.where(strict, lax.dot_general(kb.astype(bf16), k16, _NT,
                                              preferred_element_type=jnp.float32) * ldec, 0.0)
        eg = jnp.exp(g_col)
        rhs = jnp.concatenate([v * beta, kb * eg], axis=1)
        sol = _dot3(_unit_lower_inverse(m), rhs)
        u, w = sol[:, :DV_D], sol[:, DV_D:]
        s = s_ref[h]
        s16 = s.astype(bf16)
        v_new = u - jnp.dot(w.astype(bf16), s16, preferred_element_type=jnp.float32)
        att = jnp.where(tri, lax.dot_general(q.astype(bf16), k16, _NT,
                                             preferred_element_type=jnp.float32) * ldec, 0.0)
        vn16 = v_new.astype(bf16)
        o = (jnp.dot((q * eg).astype(bf16), s16, preferred_element_type=jnp.float32)
             + jnp.dot(att.astype(bf16), vn16, preferred_element_type=jnp.float32))
        g_last = g_col[L - 1:L, :]
        kdec = k * jnp.exp(g_last - g_col)
        s_ref[h] = jnp.exp(g_last) * s + lax.dot_general(kdec.astype(bf16), vn16, _TN,
                                                         preferred_element_type=jnp.float32)
        o = o * lax.rsqrt(jnp.mean(o * o, axis=-1, keepdims=True) + EPS) * ng_ref[...]
        o_ref[:, hs] = (o * _silu(gate[:, hs])).astype(o_ref.dtype)

    @pl.when(ci == n_chunks - 1)
    def _():
        sout_ref[...] = s_ref[...]


def _gdn_prompt(z2d, zt2d, batch, seq, conv_w, a_log, dt_bias, norm_g):
    L = CHUNK
    nc = seq // L
    gw = GROUP_WIDTH
    ab = jnp.zeros((2, 128), jnp.float32)
    ab = ab.at[0, H_D:2 * H_D].set(a_log.astype(jnp.float32)).at[1, H_D:2 * H_D].set(dt_bias.astype(jnp.float32))
    zspec = lambda blk: pl.BlockSpec((L, gw), lambda b, ci: (b * nc + ci, blk))
    const = lambda shape: pl.BlockSpec(shape, lambda b, ci: (0,) * len(shape))
    return pl.pallas_call(
        functools.partial(_gdn_body, n_chunks=nc),
        grid=(batch, nc),
        in_specs=[zspec(_DQ_BLK), zspec(_DQ_BLK + 1), zspec(_DQ_BLK + 2), zspec(_DG_BLK),
                  pl.BlockSpec((L, 128), lambda b, ci: (b * nc + ci, 0)),
                  const((CONV_W, W_D_QKV)), const((2, 128)), const((1, DV_D))],
        out_specs=[pl.BlockSpec((L, gw), lambda b, ci: (b * nc + ci, 0)),
                   pl.BlockSpec((None, H_D, DK_D, DV_D), lambda b, ci: (b, 0, 0, 0))],
        out_shape=[jax.ShapeDtypeStruct((batch * seq, gw), jnp.bfloat16),
                   jax.ShapeDtypeStruct((batch, H_D, DK_D, DV_D), jnp.float32)],
        scratch_shapes=[pltpu.VMEM((H_D, DK_D, DV_D), jnp.float32),
                        pltpu.VMEM((CONV_CARRY, W_D_QKV), jnp.float32)],
        compiler_params=pltpu.CompilerParams(dimension_semantics=("parallel", "arbitrary")),
    )(z2d, z2d, z2d, z2d, zt2d, conv_w.astype(jnp.float32), ab, norm_g.reshape(1, DV_D).astype(jnp.float32))


_BQ_BLK = W_A // GROUP_WIDTH
_STEP_VECS = 8


def _logaddexp(a, b):
    return jnp.maximum(a, b) + jnp.log(1.0 + jnp.exp(-jnp.abs(a - b)))


def _log_sigmoid(x):
    return jnp.minimum(x, 0.0) - jnp.log(1.0 + jnp.exp(-jnp.abs(x)))


def _rec_step_body(bq_ref, bf_ref, bi_ref, bg_ref, xq_ref, xk_ref, xv_ref, dg_ref, tail_ref, buf_ref,
                   sb_ref, sd_ref, lb_ref, cw_ref, ab_ref, nb_ref, nd_ref,
                   ob_ref, od_ref, sbo_ref, sdo_ref):
    gw = GROUP_WIDTH
    q_b = _silu(bq_ref[...]) * (DK_B ** -0.5)
    fl = bf_ref[...]
    dec_b = jnp.exp(_logaddexp(lb_ref[0:1, :], lb_ref[1:2, :] + _log_sigmoid(fl)))
    k_b = lb_ref[2:3, :] * jax.nn.sigmoid(-fl)
    v_b = bi_ref[...]
    buf = buf_ref[...]

    def conv(x_ref, part):
        cols = slice(part * gw, (part + 1) * gw)
        y = x_ref[...] * cw_ref[CONV_W - 1:CONV_W, cols]
        for j in range(CONV_W - 1):
            y = y + buf[j:j + 1, cols] * cw_ref[j:j + 1, cols]
        return _silu(y)

    qd, kd, vd = conv(xq_ref, 0), conv(xk_ref, 1), conv(xv_ref, 2)
    tail = tail_ref[...]
    beta_all = jax.nn.sigmoid(tail)
    eg_all = jnp.exp(-jnp.exp(ab_ref[0:1, :]) * _softplus(tail + ab_ref[1:2, :]))

    rows = []
    for h in range(H_B):
        hs = slice(h * DK_B, (h + 1) * DK_B)
        rows += [dec_b[:, hs], k_b[:, hs], q_b[:, hs],
                 _l2n(kd[:, hs]), _l2n(qd[:, hs]) * (DK_D ** -0.5),
                 jnp.zeros((_STEP_VECS - 5, DK_B), jnp.float32)]
    rows.append(jnp.zeros((128 - H_B * _STEP_VECS, DK_B), jnp.float32))
    cols_t = jnp.concatenate(rows, axis=0).T

    for h in range(H_B):
        hs = slice(h * DK_B, (h + 1) * DK_B)
        c0 = h * _STEP_VECS
        dec_c, k_c, q_c = cols_t[:, c0:c0 + 1], cols_t[:, c0 + 1:c0 + 2], cols_t[:, c0 + 2:c0 + 3]
        kd_c, qd_c = cols_t[:, c0 + 3:c0 + 4], cols_t[:, c0 + 4:c0 + 5]
        s_new = dec_c * sb_ref[h] + k_c * v_b[:, hs]
        sbo_ref[h] = s_new
        o = jnp.sum(q_c * s_new, axis=0, keepdims=True)
        ob_ref[:, hs] = _head_norm(o, nb_ref[...], 1.0) * _silu(bg_ref[:, hs])
        s = sd_ref[h]
        eg = eg_all[:, H_D + h:H_D + h + 1]
        beta = beta_all[:, h:h + 1]
        ks = jnp.sum(kd_c * s, axis=0, keepdims=True)
        v_new = beta * (vd[:, hs] - eg * ks)
        s_new = eg * s + kd_c * v_new
        sdo_ref[h] = s_new
        o = jnp.sum(qd_c * s_new, axis=0, keepdims=True)
        od_ref[:, hs] = _head_norm(o, nd_ref[...], 1.0) * _silu(dg_ref[:, hs])


def _rec_step(z2d, zt2d, conv_buf, s_b, s_d, lb, conv_w, a_log, dt_bias, norm_b, norm_d):
    f32 = jnp.float32
    batch = z2d.shape[0]
    gw = GROUP_WIDTH
    z3 = z2d.reshape(batch, 1, N_IN_MAIN)
    zt3 = zt2d.reshape(batch, 1, 128)
    lbf = lb.astype(f32).reshape(1, gw)
    lb_rows = jnp.concatenate([jnp.log(lbf), jnp.log1p(-lbf), 1.0 - lbf], axis=0)
    ab = jnp.zeros((2, 128), f32)
    ab = ab.at[0, H_D:2 * H_D].set(a_log.astype(f32)).at[1, H_D:2 * H_D].set(dt_bias.astype(f32))
    zspec = lambda blk: pl.BlockSpec((None, 1, gw), lambda b: (b, 0, blk))
    const = lambda shape: pl.BlockSpec(shape, lambda b: (0,) * len(shape))
    st_spec = pl.BlockSpec((None, H_B, DK_B, DV_B), lambda b: (b, 0, 0, 0))
    o_spec = pl.BlockSpec((None, 1, gw), lambda b: (b, 0, 0))
    ob, od, sbo, sdo = pl.pallas_call(
        _rec_step_body,
        grid=(batch,),
        in_specs=[zspec(_BQ_BLK), zspec(_BQ_BLK + 1), zspec(_BQ_BLK + 2), zspec(_BQ_BLK + 3),
                  zspec(_DQ_BLK), zspec(_DQ_BLK + 1), zspec(_DQ_BLK + 2), zspec(_DG_BLK),
                  pl.BlockSpec((None, 1, 128), lambda b: (b, 0, 0)),
                  pl.BlockSpec((None, CONV_W - 1, W_D_QKV), lambda b: (b, 0, 0)),
                  st_spec, st_spec,
                  const((3, gw)), const((CONV_W, W_D_QKV)), const((2, 128)),
                  const((1, DV_B)), const((1, DV_D))],
        out_specs=[o_spec, o_spec, st_spec, st_spec],
        out_shape=[jax.ShapeDtypeStruct((batch, 1, gw), f32), jax.ShapeDtypeStruct((batch, 1, gw), f32),
                   jax.ShapeDtypeStruct(s_b.shape, f32), jax.ShapeDtypeStruct(s_d.shape, f32)],
        compiler_params=pltpu.CompilerParams(dimension_semantics=("parallel",)),
    )(z3, z3, z3, z3, z3, z3, z3, z3, zt3, conv_buf.astype(f32), s_b.astype(f32), s_d.astype(f32),
      lb_rows, conv_w.astype(f32), ab, norm_b.reshape(1, DV_B).astype(f32), norm_d.reshape(1, DV_D).astype(f32))
    return ob, od, sbo, sdo


def _rms_norm(x, g):
    xf = x.astype(jnp.float32)
    y = xf * lax.rsqrt(jnp.mean(xf * xf, axis=-1, keepdims=True) + EPS)
    return (y * g.astype(jnp.float32)).astype(x.dtype)


def _l2norm(x):
    return x * lax.rsqrt(jnp.sum(x * x, axis=-1, keepdims=True) + EPS)


def _complex_affine_combine(e1, e2):
    a1r, a1i, b1r, b1i = e1
    a2r, a2i, b2r, b2i = e2
    return (a2r * a1r - a2i * a1i, a2r * a1i + a2i * a1r,
            a2r * b1r - a2i * b1i + b2r, a2r * b1i + a2i * b1r + b2i)


def _s5(u, lam_re, lam_im, log_dt, b_re, b_im, c_re, c_im, d_skip, h0_re, h0_im):
    f32 = jnp.float32
    B, T, _ = u.shape
    uf = u.astype(f32).reshape(B, T, G_A, SSM_CH)
    lr, li = lam_re.astype(f32), lam_im.astype(f32)
    dt = jnp.exp(log_dt.astype(f32))[:, None]
    mag = jnp.exp(lr * dt)
    ab_re, ab_im = mag * jnp.cos(li * dt), mag * jnp.sin(li * dt)
    den = lr * lr + li * li
    fr = ((ab_re - 1.0) * lr + ab_im * li) / den
    fi = (ab_im * lr - (ab_re - 1.0) * li) / den
    br, bi = b_re.astype(f32), b_im.astype(f32)
    bb_re = fr[..., None] * br - fi[..., None] * bi
    bb_im = fr[..., None] * bi + fi[..., None] * br
    bu_re = jnp.einsum('btgc,gpc->btgp', uf, bb_re)
    bu_im = jnp.einsum('btgc,gpc->btgp', uf, bb_im)
    a_re = jnp.broadcast_to(ab_re, bu_re.shape)
    a_im = jnp.broadcast_to(ab_im, bu_im.shape)
    acc_re, acc_im, s_re, s_im = lax.associative_scan(
        _complex_affine_combine, (a_re, a_im, bu_re, bu_im), axis=1)
    h0r = h0_re.astype(f32)[:, None]
    h0i = h0_im.astype(f32)[:, None]
    hr = s_re + acc_re * h0r - acc_im * h0i
    hi = s_im + acc_re * h0i + acc_im * h0r
    y = (jnp.einsum('gcp,btgp->btgc', c_re.astype(f32), hr)
         - jnp.einsum('gcp,btgp->btgc', c_im.astype(f32), hi)
         + d_skip.astype(f32) * uf).reshape(B, T, W_A)
    return jax.nn.gelu(y), hr[:, -1], hi[:, -1]


def _to_chunks(a, L, n):
    B, T = a.shape[:2]
    a = jnp.pad(a, [(0, 0), (0, n * L - T)] + [(0, 0)] * (a.ndim - 2))
    a = a.reshape((B, n, L) + a.shape[2:])
    a = jnp.moveaxis(a, 1, 0)
    return jnp.swapaxes(a, 2, 3)


def _from_chunks(o, T):
    n, B, H, L = o.shape[:4]
    o = jnp.moveaxis(jnp.swapaxes(o, 2, 3), 0, 1)
    return o.reshape((B, n * L) + o.shape[3:])[:, :T]


def _gla_chunked(q, k, v, logf, S0):
    T = q.shape[1]
    L = min(CHUNK, T)
    n = -(-T // L)
    qc, kc, vc, gc = (_to_chunks(a, L, n) for a in (q, k, v, logf))
    tri = jnp.tril(jnp.ones((L, L), dtype=bool))

    def step(S, inp):
        qi, ki, vi, gi = inp
        b = jnp.cumsum(gi, axis=2)
        dec = jnp.exp(jnp.where(tri[:, :, None], b[:, :, :, None, :] - b[:, :, None, :, :], -jnp.inf))
        att = jnp.sum(qi[:, :, :, None, :] * ki[:, :, None, :, :] * dec, axis=-1)
        o = (jnp.einsum('bhtk,bhkv->bhtv', qi * jnp.exp(b), S)
             + jnp.einsum('bhts,bhsv->bhtv', att, vi))
        b_last = b[:, :, -1:, :]
        S_new = (jnp.exp(b_last[:, :, 0, :])[..., None] * S
                 + jnp.einsum('bhlk,bhlv->bhkv', ki * jnp.exp(b_last - b), vi))
        return S_new, o

    S, o = lax.scan(step, S0, (qc, kc, vc, gc))
    return _from_chunks(o, T), S


def _gdn_chunked(q, k, v, beta, logg, S0):
    T = q.shape[1]
    V = v.shape[-1]
    L = min(CHUNK, T)
    n = -(-T // L)
    qc, kc, vc, bc, gc = (_to_chunks(a, L, n) for a in (q, k, v, beta, logg))
    tri = jnp.tril(jnp.ones((L, L), dtype=bool))
    strict = jnp.tril(jnp.ones((L, L), dtype=bool), -1)
    eye = jnp.eye(L, dtype=jnp.float32)

    def step(S, inp):
        qi, ki, vi, bi, gi = inp
        gcum = jnp.cumsum(gi, axis=-1)
        ldec = jnp.exp(jnp.where(tri, gcum[..., :, None] - gcum[..., None, :], -jnp.inf))
        kb = ki * bi[..., None]
        m = jnp.where(strict, jnp.einsum('bhtk,bhsk->bhts', kb, ki) * ldec, 0.0)
        rhs = jnp.concatenate([vi * bi[..., None], kb * jnp.exp(gcum)[..., None]], axis=-1)
        sol = lax.linalg.triangular_solve(eye + m, rhs, left_side=True, lower=True, unit_diagonal=True)
        u, w = sol[..., :V], sol[..., V:]
        v_new = u - jnp.einsum('bhlk,bhkv->bhlv', w, S)
        att = jnp.einsum('bhtk,bhsk->bhts', qi, ki) * ldec
        o = (jnp.einsum('bhlk,bhkv->bhlv', qi * jnp.exp(gcum)[..., None], S)
             + jnp.einsum('bhts,bhsv->bhtv', att, v_new))
        g_last = gcum[..., -1]
        S_new = (jnp.exp(g_last)[..., None, None] * S
                 + jnp.einsum('bhlk,bhlv->bhkv', ki * jnp.exp(g_last[..., None] - gcum)[..., None], v_new))
        return S_new, o

    S, o = lax.scan(step, S0, (qc, kc, vc, bc, gc))
    return _from_chunks(o, T), S


def _causal_conv(x, buf, w):
    T = x.shape[1]
    xx = jnp.concatenate([buf.astype(x.dtype), x], axis=1)
    y = xx[:, 0:T] * w[0]
    for j in range(1, CONV_W):
        y = y + xx[:, j:j + T] * w[j]
    return jax.nn.silu(y), xx[:, T:]


def _t5_bucket(qpos, kpos):
    n = jnp.maximum(qpos[:, None] - kpos[None, :], 0)
    max_exact = N_BUCKETS // 2
    nf = jnp.maximum(n, 1).astype(jnp.float32)
    large = max_exact + (jnp.log(nf / max_exact) / math.log(MAX_DISTANCE / max_exact)
                         * (N_BUCKETS - max_exact)).astype(jnp.int32)
    large = jnp.minimum(large, N_BUCKETS - 1)
    return jnp.where(n < max_exact, n, large)


def _diff_attn_core(q, k, v, qpos, kpos, lam, rel_table):
    s = jnp.einsum('bqhcd,bkhcd->bchqk', q, k) * (DH_C ** -0.5)
    bias = jnp.transpose(rel_table.astype(jnp.float32)[_t5_bucket(qpos, kpos)], (2, 0, 1))
    causal = kpos[None, :] <= qpos[:, None]
    s = jnp.where(causal, s + bias, -jnp.inf)
    p = jax.nn.softmax(s, axis=-1)
    a = p[:, 0] - lam * p[:, 1]
    return jnp.einsum('bhqk,bkhe->bqhe', a, v)


def _diff_attn_prompt(q, k, v, lam, rel_table):
    B, T = q.shape[:2]
    qb = min(Q_BLOCK, T)
    nb = T // qb
    q_blocks = jnp.moveaxis(q.reshape((B, nb, qb) + q.shape[2:]), 1, 0)
    qpos = jnp.arange(T, dtype=jnp.int32).reshape(nb, qb)
    kpos = jnp.arange(T, dtype=jnp.int32)
    out = lax.map(lambda blk: _diff_attn_core(blk[0], k, v, blk[1], kpos, lam, rel_table), (q_blocks, qpos))
    return jnp.moveaxis(out, 0, 1).reshape((B, T) + out.shape[3:])


def _split_main():
    sizes = [W_A, H_B * DK_B, H_B * DK_B, H_B * DV_B, H_B * DV_B,
             2 * H_C * DH_C, 2 * H_C * DH_C, 2 * H_C * DH_C, W_D_QKV, H_D * DV_D]
    pts, acc = [], 0
    for s in sizes[:-1]:
        acc += s
        pts.append(acc)
    return pts


def _layer(l, x, mod, lb, p, past):
    f32 = jnp.float32
    bf16 = jnp.bfloat16
    B, T, _ = x.shape
    M = B * T
    sh1, sc1, gt1, sh2, sc2, gt2 = jnp.split(mod[:, None, :], 6, axis=-1)
    Mp = max(M, 16)
    tm, tn, tk = _tiles(Mp)

    def pad_rows(a):
        return a if Mp == M else jnp.pad(a, ((0, Mp - M), (0, 0)))

    h = _rms_norm(x, p['g_norm1'][l]) * (1.0 + sc1) + sh1
    h2d = pad_rows(h.reshape(M, D_MODEL).astype(bf16))
    z = _matmul(h2d, p['w_in'], l, N_IN_MAIN, tm=tm, tn=tn, tk=tk, out_dtype=f32)
    zt = _matmul(h2d, p['w_in_tail'], l, 128, tm=tm, tn=128, tk=tk, out_dtype=f32)
    z2d = z[:M]
    z = z2d.reshape(B, T, N_IN_MAIN)
    zt = zt[:M].reshape(B, T, 128)
    (a_u, b_q, b_f, b_i, b_g, c_q, c_k, c_v, d_qkv, d_g) = jnp.split(z, _split_main(), axis=-1)
    d_beta, d_a = zt[..., :H_D], zt[..., H_D:2 * H_D]

    if past is None:
        h0_re = jnp.zeros((B, G_A, P_A), f32)
        h0_im = jnp.zeros((B, G_A, P_A), f32)
        s0_b = jnp.zeros((B, H_B, DK_B, DV_B), f32)
        s0_d = jnp.zeros((B, H_D, DK_D, DV_D), f32)
        conv0 = jnp.zeros((B, CONV_W - 1, W_D_QKV), z.dtype)
    else:
        h0_re, h0_im = past['ssm_re'][l], past['ssm_im'][l]
        s0_b, s0_d, conv0 = past['hgrn'][l], past['gdn'][l], past['conv'][l]

    s5_params = (p['ssm_lam_re'][l], p['ssm_lam_im'][l], p['ssm_log_dt'][l], p['ssm_b_re'][l],
                 p['ssm_b_im'][l], p['ssm_c_re'][l], p['ssm_c_im'][l], p['ssm_d'][l])
    if past is None:
        y_a, hr, hi = _s5_prompt(a_u, _s5_tables(*s5_params, T // S5_L))
    else:
        y_a, hr, hi = _s5_step(a_u, h0_re, h0_im, _s5_tables(*s5_params, 1, L=1))
    o_a = _s5_glu(pad_rows(y_a.reshape(M, W_A)), p['ssm_w_glu'], p['ssm_b_glu'], l,
                  min(Mp, 512))[:M].reshape(B, T, W_A)

    if past is None:
        qb_ = jax.nn.silu(b_q.astype(f32)).reshape(B, T, H_B, DK_B) * (DK_B ** -0.5)
        fl = b_f.astype(f32).reshape(B, T, H_B, DK_B)
        lbh = lb.reshape(H_B, DK_B)
        logf = jnp.logaddexp(jnp.log(lbh), jnp.log1p(-lbh) + jax.nn.log_sigmoid(fl))
        kb_ = (1.0 - lbh) * jax.nn.sigmoid(-fl)
        vb_ = b_i.astype(f32).reshape(B, T, H_B, DV_B)
        o_b, s_b = _gla_chunked(qb_, kb_, vb_, logf, s0_b.astype(f32))
        o_b = (_rms_norm(o_b, p['hgrn_norm'][l]) * jax.nn.silu(b_g.astype(f32).reshape(B, T, H_B, DV_B))).reshape(B, T, H_B * DV_B)
    else:
        o_b, o_d, s_b, s_d = _rec_step(z2d, zt[:, 0, :], conv0, s0_b, s0_d, lb, p['gdn_conv'][l],
                                       p['gdn_a_log'][l], p['gdn_dt_bias'][l], p['hgrn_norm'][l],
                                       p['gdn_norm'][l])

    lam_init = 0.8 - 0.6 * math.exp(-0.3 * l)
    lam = (jnp.exp(jnp.sum(p['diff_lq1'][l].astype(f32) * p['diff_lk1'][l].astype(f32)))
           - jnp.exp(jnp.sum(p['diff_lq2'][l].astype(f32) * p['diff_lk2'][l].astype(f32))) + lam_init)
    kh = c_k.reshape(B, T, H_C, 2 * DH_C)
    vh = c_v.reshape(B, T, H_C, 2 * DH_C)
    if past is None:
        o_c = _diff_attn_prompt_pallas(z2d, B, T, lam, _attn_bias_tiles(p['rel_bias']),
                                       p['diff_norm'][l], 1.0 - lam_init)
    else:
        o_c = _diff_attn_step_pallas(c_q.reshape(M, -1), c_k.reshape(M, -1), c_v.reshape(M, -1),
                                     past['cache_k'], past['cache_v'], past['page_table'], l, lam,
                                     p['rel_bias'], p['diff_norm'][l], 1.0 - lam_init)
    o_c = o_c.reshape(B, T, 2 * H_C * DH_C)

    conv_new = jnp.concatenate([conv0.astype(d_qkv.dtype), d_qkv], axis=1)[:, T:]
    if past is None:
        o_d, s_d = _gdn_prompt(z2d, zt.reshape(M, 128), B, T, p['gdn_conv'][l], p['gdn_a_log'][l],
                               p['gdn_dt_bias'][l], p['gdn_norm'][l])
    o_b = o_b.reshape(B, T, H_B * DV_B)
    o_d = o_d.reshape(B, T, H_D * DV_D)

    o = pad_rows(jnp.concatenate([o_a, o_b, o_c, o_d], axis=-1).reshape(M, D_MODEL).astype(bf16))
    x2d = pad_rows(x.reshape(M, D_MODEL))

    def gate_rows(g):
        if T < tm:
            return pad_rows(jnp.broadcast_to(g, (B, T, D_MODEL)).reshape(M, D_MODEL)), \
                pl.BlockSpec((tm, tn), lambda i, j, k: (i, j))
        per = T // tm
        return g.reshape(B, 1, D_MODEL), \
            pl.BlockSpec((None, 1, tn), lambda i, j, k: (i // per, 0, j))

    g1, g1_spec = gate_rows(gt1)
    x2d = _matmul(o, p['w_out'], l, D_MODEL, tm=tm, tn=tn, tk=tk, out_dtype=f32,
                  epilogue="gated_residual", extras=(x2d, g1),
                  extra_specs=(pl.BlockSpec((tm, tn), lambda i, j, k: (i, j)), g1_spec))

    xm = x2d[:M].reshape(B, T, D_MODEL)
    h2 = pad_rows((_rms_norm(xm, p['g_norm2'][l]) * (1.0 + sc2) + sh2).reshape(M, D_MODEL).astype(bf16))
    ff = _matmul(h2, p['w_up'], l, D_FF, tm=tm, tn=tn, tk=tk, out_dtype=bf16, epilogue="relu2")
    g2, g2_spec = gate_rows(gt2)
    x2d = _matmul(ff, p['w_down'], l, D_MODEL, tm=tm, tn=tn, tk=tk, out_dtype=f32,
                  epilogue="gated_residual", extras=(x2d, g2),
                  extra_specs=(pl.BlockSpec((tm, tn), lambda i, j, k: (i, j)), g2_spec))
    return x2d[:M].reshape(B, T, D_MODEL), (kh, vh, hr, hi, s_b, s_d, conv_new)


def _trunk(x, mods, p, past):
    sm = jax.nn.softmax(p['hgrn_lb_logits'].astype(jnp.float32), axis=0)
    lb_all = jnp.cumsum(sm, axis=0)
    lb_all = lb_all - lb_all[0:1]
    cols = [[] for _ in range(7)]
    for l in range(DEPTH):
        x, st = _layer(l, x, mods[l], lb_all[l], p, past)
        for j in range(7):
            cols[j].append(st[j])
    y = _rms_norm(x, p['g_final'])
    return y, [jnp.stack(col, axis=0) for col in cols]


def kernel(x_prompt, x_sample, c_prompt, c_sample, cache_k, cache_v, page_table, state_ssm_re, state_ssm_im, state_hgrn, state_gdn, state_gdn_conv, w_ada, b_ada, g_norm1, w_in, ssm_lam_re, ssm_lam_im, ssm_log_dt, ssm_b_re, ssm_b_im, ssm_c_re, ssm_c_im, ssm_d, ssm_w_glu, ssm_b_glu, hgrn_lb_logits, hgrn_norm, diff_lq1, diff_lk1, diff_lq2, diff_lk2, diff_norm, rel_bias, gdn_conv, gdn_a_log, gdn_dt_bias, gdn_norm, w_out, g_norm2, w_up, w_down, g_final):
    w_in_tail = jnp.pad(w_in[:, :, N_IN_MAIN:], ((0, 0), (0, 0), (0, 128 - N_TAIL)))
    p = dict(w_ada=w_ada, b_ada=b_ada, g_norm1=g_norm1, w_in=w_in, w_in_tail=w_in_tail,
             ssm_lam_re=ssm_lam_re, ssm_lam_im=ssm_lam_im, ssm_log_dt=ssm_log_dt,
             ssm_b_re=ssm_b_re, ssm_b_im=ssm_b_im, ssm_c_re=ssm_c_re, ssm_c_im=ssm_c_im,
             ssm_d=ssm_d, ssm_w_glu=ssm_w_glu, ssm_b_glu=ssm_b_glu,
             hgrn_lb_logits=hgrn_lb_logits, hgrn_norm=hgrn_norm,
             diff_lq1=diff_lq1, diff_lk1=diff_lk1, diff_lq2=diff_lq2, diff_lk2=diff_lk2,
             diff_norm=diff_norm, rel_bias=rel_bias,
             gdn_conv=gdn_conv, gdn_a_log=gdn_a_log, gdn_dt_bias=gdn_dt_bias, gdn_norm=gdn_norm,
             w_out=w_out, g_norm2=g_norm2, w_up=w_up, w_down=w_down, g_final=g_final)

    nb_p, nb_s = c_prompt.shape[0], c_sample.shape[0]
    c_all = jnp.concatenate([c_prompt, c_sample], axis=0)
    rows = 16
    cs = jnp.pad(jax.nn.silu(c_all), ((0, rows - nb_p - nb_s), (0, 0))).astype(jnp.bfloat16)
    mods_p, mods_s = [], []
    for l in range(DEPTH):
        m = _matmul(cs, w_ada, l, 6 * D_MODEL, tm=rows, tn=2048, tk=1024, out_dtype=jnp.float32,
                    epilogue="bias", extras=(b_ada[l][None, :],),
                    extra_specs=(pl.BlockSpec((1, 2048), lambda i, j, k: (0, j)),))
        mods_p.append(m[:nb_p])
        mods_s.append(m[nb_p:nb_p + nb_s])

    y_prompt, sp = _trunk(x_prompt, mods_p, p, None)

    past = dict(cache_k=cache_k, cache_v=cache_v, page_table=page_table,
                ssm_re=state_ssm_re, ssm_im=state_ssm_im, hgrn=state_hgrn,
                gdn=state_gdn, conv=state_gdn_conv)
    y_sample, ss = _trunk(x_sample, mods_s, p, past)
    return (y_prompt, y_sample,
            sp[0], sp[1], sp[2], sp[3], sp[4], sp[5], sp[6],
            ss[0], ss[1], ss[2], ss[3], ss[4], ss[5], ss[6])
```

```python
import functools
import math

import jax
import jax.numpy as jnp
from jax import lax
from jax.experimental import pallas as pl
from jax.experimental.pallas import tpu as pltpu

D_MODEL = 4096
DEPTH = 4
PAGE_SIZE = 128
GROUP_WIDTH = D_MODEL // 4
SSM_CH = 16
G_A = GROUP_WIDTH // SSM_CH
P_A = 64
W_A = G_A * SSM_CH
H_B = 8
DK_B = 128
DV_B = GROUP_WIDTH // H_B
H_C = 4
DH_C = GROUP_WIDTH // (2 * H_C)
N_BUCKETS = 32
MAX_DISTANCE = 128
Q_BLOCK = 128
H_D = 8
DK_D = 128
DV_D = GROUP_WIDTH // H_D
CONV_W = 4
W_D_QKV = H_D * (2 * DK_D + DV_D)
CHUNK = 64
D_FF = 4 * D_MODEL
N_IN = W_A + 2 * H_B * DK_B + 2 * H_B * DV_B + 6 * H_C * DH_C + W_D_QKV + H_D * DV_D + 2 * H_D
N_IN_MAIN = (N_IN // 128) * 128
N_TAIL = N_IN - N_IN_MAIN
EPS = 1e-6

VMEM_LIMIT_BYTES = 56 * 1024 * 1024


def _mm_epilogue(acc, extra_refs, epilogue):
    if epilogue == "none":
        return acc
    if epilogue == "bias":
        return acc + extra_refs[0][...]
    if epilogue == "relu2":
        r = jnp.maximum(acc, 0.0)
        return r * r
    if epilogue == "gated_residual":
        return extra_refs[0][...] + extra_refs[1][...] * acc
    raise ValueError(epilogue)


def _mm_body(a_ref, w_ref, *rest, nk, epilogue):
    prod = jnp.dot(a_ref[...], w_ref[...].astype(jnp.bfloat16), preferred_element_type=jnp.float32)
    if nk == 1:
        *extra_refs, o_ref = rest
        o_ref[...] = _mm_epilogue(prod, extra_refs, epilogue).astype(o_ref.dtype)
        return
    *extra_refs, o_ref, acc_ref = rest
    k = pl.program_id(2)

    @pl.when(k == 0)
    def _():
        acc_ref[...] = prod

    @pl.when((k > 0) & (k < nk - 1))
    def _():
        acc_ref[...] += prod

    @pl.when(k == nk - 1)
    def _():
        o_ref[...] = _mm_epilogue(acc_ref[...] + prod, extra_refs, epilogue).astype(o_ref.dtype)


def _matmul(a, w, layer, n_cols, *, tm, tn, tk, out_dtype, epilogue="none",
            extras=(), extra_specs=()):
    m, k_dim = a.shape
    assert w.shape[1] == k_dim and m % tm == 0 and n_cols % tn == 0 and k_dim % tk == 0
    nk = k_dim // tk
    grid = (m // tm, n_cols // tn, nk)
    in_specs = [
        pl.BlockSpec((tm, tk), lambda i, j, k: (i, k)),
        pl.BlockSpec((None, tk, tn), lambda i, j, k: (layer, k, j)),
    ] + list(extra_specs)
    return pl.pallas_call(
        functools.partial(_mm_body, nk=nk, epilogue=epilogue),
        grid=grid,
        in_specs=in_specs,
        out_specs=pl.BlockSpec((tm, tn), lambda i, j, k: (i, j)),
        out_shape=jax.ShapeDtypeStruct((m, n_cols), out_dtype),
        scratch_shapes=[pltpu.VMEM((tm, tn), jnp.float32)] if nk > 1 else [],
        compiler_params=pltpu.CompilerParams(
            dimension_semantics=("parallel", "parallel", "arbitrary"),
            vmem_limit_bytes=VMEM_LIMIT_BYTES),
    )(a, w, *extras)


def _tiles(m):
    if m >= 1024:
        return 1024, 512, 4096
    return m, 2048, 1024


def _split_bf16(x):
    hi = x.astype(jnp.bfloat16)
    lo = (x - hi.astype(jnp.float32)).astype(jnp.bfloat16)
    return hi, lo


def _dot3(a, b, dims=(((1,), (0,)), ((), ()))):
    ah, al = _split_bf16(a)
    bh, bl = _split_bf16(b)

    def d(x, y):
        return lax.dot_general(x, y, dims, preferred_element_type=jnp.float32)

    return d(ah, bh) + (d(ah, bl) + d(al, bh))


_NT = (((1,), (1,)), ((), ()))
_TN = (((0,), (0,)), ((), ()))


S5_L = 16


def _s5_tables(lam_re, lam_im, log_dt, b_re, b_im, c_re, c_im, d_skip, n_chunks, L=S5_L):
    f32 = jnp.float32
    hp = lax.Precision.HIGHEST
    G, P, C = G_A, P_A, SSM_CH
    lr, li = lam_re.astype(f32), lam_im.astype(f32)
    dt = jnp.exp(log_dt.astype(f32))[:, None]
    mag = jnp.exp(lr * dt)
    ab_re, ab_im = mag * jnp.cos(li * dt), mag * jnp.sin(li * dt)
    den = lr * lr + li * li
    fr = ((ab_re - 1.0) * lr + ab_im * li) / den
    fi = (ab_im * lr - (ab_re - 1.0) * li) / den
    br, bi = b_re.astype(f32), b_im.astype(f32)
    bb_re = fr[..., None] * br - fi[..., None] * bi
    bb_im = fr[..., None] * bi + fi[..., None] * br

    def powers(ns):
        nf = jnp.asarray(ns, f32)[None, :, None]
        m = jnp.exp(nf * (lr * dt)[:, None, :])
        ph = nf * (li * dt)[:, None, :]
        return m * jnp.cos(ph), m * jnp.sin(ph)

    ar, ai = powers(list(range(L + 1)))
    cr, ci = c_re.astype(f32), c_im.astype(f32)
    car = cr[:, None] * ar[:, :, None, :] - ci[:, None] * ai[:, :, None, :]
    cai = cr[:, None] * ai[:, :, None, :] + ci[:, None] * ar[:, :, None, :]
    kk = (jnp.einsum('gncp,gpd->gncd', car[:, :L], bb_re, precision=hp)
          - jnp.einsum('gncp,gpd->gncd', cai[:, :L], bb_im, precision=hp))
    tl = jnp.arange(L)
    lag = tl[:, None] - tl[None, :]
    kfull = kk[:, jnp.clip(lag, 0, L - 1)]
    kfull = jnp.where((lag >= 0)[None, :, :, None, None], kfull, 0.0)
    wt = jnp.transpose(kfull, (0, 2, 4, 1, 3)).reshape(G, L * C, L * C)
    arr, aii = ar[:, L - 1 - tl], ai[:, L - 1 - tl]
    bm_re = arr[:, :, None, :] * jnp.swapaxes(bb_re, 1, 2)[:, None] - aii[:, :, None, :] * jnp.swapaxes(bb_im, 1, 2)[:, None]
    bm_im = arr[:, :, None, :] * jnp.swapaxes(bb_im, 1, 2)[:, None] + aii[:, :, None, :] * jnp.swapaxes(bb_re, 1, 2)[:, None]
    bmt = jnp.concatenate([bm_re, bm_im], axis=-1).reshape(G, L * C, 2 * P)
    cm_re = jnp.transpose(car[:, 1:], (0, 3, 1, 2))
    cm_im = -jnp.transpose(cai[:, 1:], (0, 3, 1, 2))
    cmt = jnp.concatenate([cm_re, cm_im], axis=1).reshape(G, 2 * P, L * C)
    n_steps = max(1, (n_chunks - 1).bit_length())
    sr, si = powers([L * (1 << i) for i in range(n_steps)])
    ca = jnp.concatenate([sr, sr], axis=-1)
    cb = jnp.concatenate([-si, si], axis=-1)
    dtile = jnp.tile(d_skip.astype(f32), (1, L))[:, None, :]
    return wt, bmt, cmt, ca, cb, dtile, n_steps


def _s5_body(x_ref, wt_ref, bm_ref, cm_ref, ca_ref, cb_ref, d_ref, y_ref, st_ref, *,
             n_chunks, n_steps, batch):
    x = x_ref[...]
    rows = x.shape[0]
    e = _dot3(x, bm_ref[...])
    jidx = lax.broadcasted_iota(jnp.int32, (rows, 2 * P_A), 0) % n_chunks
    for i in range(n_steps):
        sh = 1 << i
        xs = jnp.where(jidx >= sh, pltpu.roll(e, sh, axis=0), 0.0)
        e = e + xs * ca_ref[i:i + 1, :] + pltpu.roll(xs, P_A, axis=1) * cb_ref[i:i + 1, :]
    hprev = jnp.where(jidx >= 1, pltpu.roll(e, 1, axis=0), 0.0)
    y_ref[...] = _dot3(x, wt_ref[...]) + _dot3(hprev, cm_ref[...]) + x * d_ref[...]
    for b in range(batch):
        r = b * n_chunks + n_chunks - 1
        st_ref[b:b + 1, :] = e[r:r + 1, :]


def _s5_prompt(a_u, tables):
    wt, bmt, cmt, ca, cb, dtile, n_steps = tables
    B, T, _ = a_u.shape
    L, G, C, P = S5_L, G_A, SSM_CH, P_A
    nc = T // L
    rows = B * nc
    x = a_u.reshape(B, nc, L, G, C).transpose(3, 0, 1, 2, 4).reshape(G, rows, L * C)
    full = lambda *shape: pl.BlockSpec((None,) + shape, lambda g: (g,) + (0,) * len(shape))
    y, st = pl.pallas_call(
        functools.partial(_s5_body, n_chunks=nc, n_steps=n_steps, batch=B),
        grid=(G,),
        in_specs=[full(rows, L * C), full(L * C, L * C), full(L * C, 2 * P), full(2 * P, L * C),
                  full(n_steps, 2 * P), full(n_steps, 2 * P), full(1, L * C)],
        out_specs=[full(rows, L * C), full(B, 2 * P)],
        out_shape=[jax.ShapeDtypeStruct((G, rows, L * C), jnp.float32),
                   jax.ShapeDtypeStruct((G, B, 2 * P), jnp.float32)],
        compiler_params=pltpu.CompilerParams(dimension_semantics=("parallel",)),
    )(x, wt, bmt, cmt, ca, cb, dtile)
    y = y.reshape(G, B, nc, L, C).transpose(1, 2, 3, 0, 4).reshape(B, T, G * C)
    st = jnp.swapaxes(st, 0, 1)
    return y, st[..., :P], st[..., P:]


def _s5_step_body(x_ref, h_ref, w_ref, bm_ref, cm_ref, ca_ref, cb_ref, d_ref, y_ref, st_ref):
    x = x_ref[...]
    h0 = h_ref[...]
    st_ref[...] = (_dot3(x, bm_ref[...]) + h0 * ca_ref[...]
                   + pltpu.roll(h0, P_A, axis=1) * cb_ref[...])
    y_ref[...] = _dot3(x, w_ref[...]) + _dot3(h0, cm_ref[...]) + x * d_ref[...]


def _s5_step(a_u, h0_re, h0_im, tables):
    wt, bmt, cmt, ca, cb, dtile, _ = tables
    B = a_u.shape[0]
    G, C, P = G_A, SSM_CH, P_A
    x = a_u.reshape(B, G, C).transpose(1, 0, 2)
    h0 = jnp.concatenate([h0_re, h0_im], axis=-1).transpose(1, 0, 2)
    full = lambda *shape: pl.BlockSpec((None,) + shape, lambda g: (g,) + (0,) * len(shape))
    y, st = pl.pallas_call(
        _s5_step_body,
        grid=(G,),
        in_specs=[full(B, C), full(B, 2 * P), full(C, C), full(C, 2 * P), full(2 * P, C),
                  full(1, 2 * P), full(1, 2 * P), full(1, C)],
        out_specs=[full(B, C), full(B, 2 * P)],
        out_shape=[jax.ShapeDtypeStruct((G, B, C), jnp.float32),
                   jax.ShapeDtypeStruct((G, B, 2 * P), jnp.float32)],
        compiler_params=pltpu.CompilerParams(dimension_semantics=("parallel",)),
    )(x, h0.astype(jnp.float32), wt, bmt, cmt, ca, cb, dtile)
    y = y.transpose(1, 0, 2).reshape(B, 1, G * C)
    st = jnp.swapaxes(st, 0, 1)
    return y, st[..., :P], st[..., P:]


def _gelu_tanh(x):
    c = math.sqrt(2.0 / math.pi)
    return 0.5 * x * (1.0 + jnp.tanh(c * (x + 0.044715 * (x * x * x))))


def _glu_body(y_ref, w_ref, b_ref, o_ref):
    zg = _gelu_tanh(y_ref[...])
    glu = jnp.dot(zg.astype(jnp.bfloat16), w_ref[...].astype(jnp.bfloat16),
                  preferred_element_type=jnp.float32) + b_ref[...]
    o_ref[...] = (zg * jax.nn.sigmoid(glu)).astype(o_ref.dtype)


def _s5_glu(y2d, w_glu, b_glu, layer, tm):
    m = y2d.shape[0]
    return pl.pallas_call(
        _glu_body,
        grid=(m // tm,),
        in_specs=[pl.BlockSpec((tm, W_A), lambda i: (i, 0)),
                  pl.BlockSpec((None, W_A, W_A), lambda i: (layer, 0, 0)),
                  pl.BlockSpec((None, 1, W_A), lambda i: (layer, 0, 0))],
        out_specs=pl.BlockSpec((tm, W_A), lambda i: (i, 0)),
        out_shape=jax.ShapeDtypeStruct((m, W_A), jnp.bfloat16),
        compiler_params=pltpu.CompilerParams(dimension_semantics=("parallel",),
                                             vmem_limit_bytes=VMEM_LIMIT_BYTES),
    )(y2d, w_glu, b_glu.reshape(DEPTH, 1, W_A))


ATT_T = 256
MASKED = -1e30
_CQ_BLK = (W_A + 2 * H_B * DK_B + 2 * H_B * DV_B) // (2 * DH_C)
_CK_BLK = _CQ_BLK + H_C
_CV_BLK = _CK_BLK + H_C


def _attn_bias_tiles(rel_bias):
    t = ATT_T
    assert t >= MAX_DISTANCE
    qpos = jnp.arange(3 * t, dtype=jnp.int32)
    kpos = jnp.arange(t, dtype=jnp.int32)
    bias = rel_bias.astype(jnp.float32)[_t5_bucket(qpos, kpos)]
    bias = jnp.where((kpos[None, :] <= qpos[:, None])[..., None], bias, MASKED)
    return bias.transpose(2, 0, 1).reshape(H_C, 3, t, t)


def _softmax_update(c, s, v_bf16, m_ref, l_ref, acc_ref):
    m_prev = m_ref[c]
    m_new = jnp.maximum(m_prev, jnp.max(s, axis=-1, keepdims=True))
    alpha = jnp.exp(m_prev - m_new)
    p = jnp.exp(s - m_new)
    l_ref[c] = alpha * l_ref[c] + jnp.sum(p, axis=-1, keepdims=True)
    acc_ref[c] = alpha * acc_ref[c] + jnp.dot(p.astype(jnp.bfloat16), v_bf16,
                                              preferred_element_type=jnp.float32)
    m_ref[c] = m_new


def _head_norm(o, g, out_scale):
    return o * lax.rsqrt(jnp.mean(o * o, axis=-1, keepdims=True) + EPS) * g * out_scale


def _attn_body(lam_ref, q_ref, k_ref, v_ref, bias_ref, g_ref, o_ref, m_ref, l_ref, acc_ref, *,
               out_scale):
    qi = pl.program_id(2)
    ki = pl.program_id(3)

    @pl.when(ki == 0)
    def _():
        m_ref[...] = jnp.full_like(m_ref, MASKED)
        l_ref[...] = jnp.zeros_like(l_ref)
        acc_ref[...] = jnp.zeros_like(acc_ref)

    @pl.when(ki <= qi)
    def _():
        q = q_ref[...].astype(jnp.bfloat16)
        k = k_ref[...].astype(jnp.bfloat16)
        v = v_ref[...].astype(jnp.bfloat16)
        bias = bias_ref[...]
        for c in range(2):
            s = lax.dot_general(q[:, c * DH_C:(c + 1) * DH_C], k[:, c * DH_C:(c + 1) * DH_C], _NT,
                                preferred_element_type=jnp.float32) * (DH_C ** -0.5) + bias
            _softmax_update(c, s, v, m_ref, l_ref, acc_ref)

    @pl.when(ki == qi)
    def _():
        o = acc_ref[0] / l_ref[0] - lam_ref[...] * (acc_ref[1] / l_ref[1])
        o_ref[...] = _head_norm(o, g_ref[...], out_scale).astype(o_ref.dtype)


def _diff_attn_prompt_pallas(z2d, batch, seq, lam, bias_tiles, g_norm, out_scale):
    t = ATT_T
    nq = seq // t
    w = 2 * DH_C
    grid = (batch, H_C, nq, nq)
    return pl.pallas_call(
        functools.partial(_attn_body, out_scale=out_scale),
        grid=grid,
        in_specs=[
            pl.BlockSpec((1, 1), lambda b, h, qi, ki: (0, 0)),
            pl.BlockSpec((t, w), lambda b, h, qi, ki: (b * nq + qi, _CQ_BLK + h)),
            pl.BlockSpec((t, w), lambda b, h, qi, ki: (b * nq + jnp.minimum(ki, qi), _CK_BLK + h)),
            pl.BlockSpec((t, w), lambda b, h, qi, ki: (b * nq + jnp.minimum(ki, qi), _CV_BLK + h)),
            pl.BlockSpec((None, None, t, t),
                         lambda b, h, qi, ki: (h, jnp.clip(qi - ki, 0, 2), 0, 0)),
            pl.BlockSpec((1, w), lambda b, h, qi, ki: (0, 0)),
        ],
        out_specs=pl.BlockSpec((t, w), lambda b, h, qi, ki: (b * nq + qi, h)),
        out_shape=jax.ShapeDtypeStruct((batch * seq, H_C * w), jnp.bfloat16),
        scratch_shapes=[pltpu.VMEM((2, t, 1), jnp.float32), pltpu.VMEM((2, t, 1), jnp.float32),
                        pltpu.VMEM((2, t, w), jnp.float32)],
        compiler_params=pltpu.CompilerParams(
            dimension_semantics=("parallel", "parallel", "parallel", "arbitrary")),
    )(lam.reshape(1, 1), z2d, z2d, z2d, bias_tiles, g_norm.reshape(1, w))


def _attn_step_body(pt_ref, lam_ref, q_ref, kn_ref, vn_ref, kc_ref, vc_ref, bias_ref, bias0_ref, g_ref,
                    o_ref, m_ref, l_ref, acc_ref, qrows_ref, *, n_pages, out_scale):
    del pt_ref
    p = pl.program_id(1)
    w = 2 * DH_C
    nrow = qrows_ref.shape[1]
    scale = DH_C ** -0.5

    @pl.when(p == 0)
    def _():
        col = lax.broadcasted_iota(jnp.int32, (nrow, w), 1)
        row = lax.broadcasted_iota(jnp.int32, (nrow, w), 0)
        for h in range(H_C):
            hs = slice(h * w, (h + 1) * w)
            qrows = jnp.where(col // DH_C == row, q_ref[:, hs], 0.0)
            qrows_ref[h] = qrows.astype(jnp.bfloat16)
            s_new = jnp.sum(qrows * kn_ref[:, hs], axis=-1, keepdims=True) * scale + bias0_ref[h]
            m_ref[h] = s_new
            l_ref[h] = jnp.ones_like(s_new)
            acc_ref[h] = jnp.broadcast_to(vn_ref[:, hs], (nrow, w))

    for h in range(H_C):
        k = kc_ref[:, h, :].astype(jnp.bfloat16)
        v = vc_ref[:, h, :].astype(jnp.bfloat16)
        s = lax.dot_general(qrows_ref[h], k, _NT, preferred_element_type=jnp.float32) * scale + bias_ref[h]
        _softmax_update(h, s, v, m_ref, l_ref, acc_ref)

    @pl.when(p == n_pages - 1)
    def _():
        for h in range(H_C):
            a = acc_ref[h] / l_ref[h]
            o = a[0:1, :] - lam_ref[...] * a[1:2, :]
            o_ref[:, h * w:(h + 1) * w] = _head_norm(o, g_ref[...], out_scale)


def _diff_attn_step_pallas(q, k_new, v_new, cache_k, cache_v, page_table, layer, lam, rel_bias,
                           g_norm, out_scale):
    f32 = jnp.float32
    batch, n_pages = page_table.shape
    w = 2 * DH_C
    wide = H_C * w
    nrow = 16
    past_len = n_pages * PAGE_SIZE
    bucket = _t5_bucket(jnp.full((1,), past_len, jnp.int32), jnp.arange(past_len + 1, dtype=jnp.int32))[0]
    bias_h = rel_bias.astype(f32)[bucket].T.reshape(H_C, 1, past_len + 1)
    bias_past, bias0 = bias_h[:, :, :past_len], bias_h[:, :, past_len:]
    vec = lambda a: a.reshape(batch, 1, wide).astype(f32)
    row_spec = pl.BlockSpec((None, 1, wide), lambda b, p, pt: (b, 0, 0))
    page_spec = pl.BlockSpec((None, None, PAGE_SIZE, H_C, w), lambda b, p, pt: (layer, pt[b, p], 0, 0, 0))
    return pl.pallas_call(
        functools.partial(_attn_step_body, n_pages=n_pages, out_scale=out_scale),
        grid_spec=pltpu.PrefetchScalarGridSpec(
            num_scalar_prefetch=1,
            grid=(batch, n_pages),
            in_specs=[
                pl.BlockSpec((1, 1), lambda b, p, pt: (0, 0)),
                row_spec, row_spec, row_spec,
                page_spec, page_spec,
                pl.BlockSpec((H_C, 1, PAGE_SIZE), lambda b, p, pt: (0, 0, p)),
                pl.BlockSpec((H_C, 1, 1), lambda b, p, pt: (0, 0, 0)),
                pl.BlockSpec((1, w), lambda b, p, pt: (0, 0)),
            ],
            out_specs=pl.BlockSpec((None, 1, wide), lambda b, p, pt: (b, 0, 0)),
            scratch_shapes=[pltpu.VMEM((H_C, nrow, 1), f32), pltpu.VMEM((H_C, nrow, 1), f32),
                            pltpu.VMEM((H_C, nrow, w), f32), pltpu.VMEM((H_C, nrow, w), jnp.bfloat16)],
        ),
        out_shape=jax.ShapeDtypeStruct((batch, 1, wide), f32),
        compiler_params=pltpu.CompilerParams(dimension_semantics=("parallel", "arbitrary")),
    )(page_table, lam.reshape(1, 1), vec(q), vec(k_new), vec(v_new), cache_k, cache_v, bias_past, bias0,
      g_norm.reshape(1, w))


_DQ_BLK = (N_IN_MAIN - H_D * DV_D - W_D_QKV) // GROUP_WIDTH
_DG_BLK = (N_IN_MAIN - H_D * DV_D) // GROUP_WIDTH
CONV_CARRY = 8


def _silu(x):
    return x * jax.nn.sigmoid(x)


def _softplus(x):
    return jnp.maximum(x, 0.0) + jnp.log(1.0 + jnp.exp(-jnp.abs(x)))


def _cumsum_rows(x):
    n = x.shape[0]
    row = lax.broadcasted_iota(jnp.int32, x.shape, 0)
    sh = 1
    while sh < n:
        x = x + jnp.where(row >= sh, pltpu.roll(x, sh, axis=0), 0.0)
        sh *= 2
    return x


def _unit_lower_inverse(ms):
    n = ms[0].shape[0]
    r = lax.broadcasted_iota(jnp.int32, (n, n), 0)
    c = lax.broadcasted_iota(jnp.int32, (n, n), 1)
    eye = (r == c).astype(jnp.float32)
    ts = [eye - jnp.where(r // 2 == c // 2, m, 0.0) for m in ms]
    sz = 2
    while sz < n:
        off = (r // (2 * sz) == c // (2 * sz)) & (r // sz != c // sz)
        tm = [_dot3(t, jnp.where(off, m, 0.0)) for t, m in zip(ts, ms)]
        ts = [t - _dot3(x, t) for t, x in zip(ts, tm)]
        sz *= 2
    return ts


def _l2n(x):
    return x * lax.rsqrt(jnp.sum(x * x, axis=-1, keepdims=True) + EPS)


def _gdn_body(xq_ref, xk_ref, xv_ref, gate_ref, tail_ref, cw_ref, ab_ref, ng_ref,
              o_ref, sout_ref, s_ref, carry_ref, *, n_chunks):
    bf16 = jnp.bfloat16
    L = CHUNK
    ci = pl.program_id(1)

    @pl.when(ci == 0)
    def _():
        s_ref[...] = jnp.zeros_like(s_ref)
        carry_ref[...] = jnp.zeros_like(carry_ref)

    def conv(x_ref, part):
        x = x_ref[...]
        cols = slice(part * GROUP_WIDTH, (part + 1) * GROUP_WIDTH)
        xc = jnp.concatenate([carry_ref[:, cols], x], axis=0)
        w = cw_ref[:, cols]
        base = CONV_CARRY - (CONV_W - 1)
        y = xc[base:base + L] * w[0:1]
        for j in range(1, CONV_W):
            y = y + xc[base + j:base + j + L] * w[j:j + 1]
        carry_ref[:, cols] = x[L - CONV_CARRY:L]
        return _silu(y)

    qc, kc, vc = conv(xq_ref, 0), conv(xk_ref, 1), conv(xv_ref, 2)
    gate = gate_ref[...]
    tail = tail_ref[...]
    beta_all = jax.nn.sigmoid(tail)
    logg = -jnp.exp(ab_ref[0:1, :]) * _softplus(tail + ab_ref[1:2, :])
    gcum_all = _cumsum_rows(logg)
    gcum_t = jnp.concatenate([gcum_all, jnp.zeros((128 - L, 128), jnp.float32)], axis=0).T[:, :L]

    r = lax.broadcasted_iota(jnp.int32, (L, L), 0)
    c = lax.broadcasted_iota(jnp.int32, (L, L), 1)
    tri = r >= c
    strict = r > c

    heads = range(H_D)
    hsl = [slice(h * DK_D, (h + 1) * DK_D) for h in heads]
    states = [s_ref[h] for h in heads]
    q = [_l2n(qc[:, hsl[h]]) * (DK_D ** -0.5) for h in heads]
    k = [_l2n(kc[:, hsl[h]]) for h in heads]
    beta = [beta_all[:, h:h + 1] for h in heads]
    g_col = [gcum_all[:, H_D + h:H_D + h + 1] for h in heads]
    ldec = [jnp.exp(jnp.minimum(g_col[h] - gcum_t[H_D + h:H_D + h + 1, :], 0.0)) for h in heads]
    kb = [k[h] * beta[h] for h in heads]
    k16 = [k[h].astype(bf16) for h in heads]
    m = [jnp.where(strict, lax.dot_general(kb[h].astype(bf16), k16[h], _NT,
                                           preferred_element_type=jnp.float32) * ldec[h], 0.0)
         for h in heads]
    eg = [jnp.exp(g_col[h]) for h in heads]
    rhs = [jnp.concatenate([vc[:, hsl[h]] * beta[h], kb[h] * eg[h]], axis=1) for h in heads]
    tinv = _unit_lower_inverse(m)
    sol = [_dot3(tinv[h], rhs[h]) for h in heads]
    s16 = [states[h].astype(bf16) for h in heads]
    v_new = [sol[h][:, :DV_D] - jnp.dot(sol[h][:, DV_D:].astype(bf16), s16[h],
                                        preferred_element_type=jnp.float32) for h in heads]
    att = [jnp.where(tri, lax.dot_general(q[h].astype(bf16), k16[h], _NT,
                                          preferred_element_type=jnp.float32) * ldec[h], 0.0)
           for h in heads]
    vn16 = [v_new[h].astype(bf16) for h in heads]
    o = [jnp.dot((q[h] * eg[h]).astype(bf16), s16[h], preferred_element_type=jnp.float32)
         + jnp.dot(att[h].astype(bf16), vn16[h], preferred_element_type=jnp.float32) for h in heads]
    for h in heads:
        g_last = g_col[h][L - 1:L, :]
        kdec = k[h] * jnp.exp(g_last - g_col[h])
        s_ref[h] = jnp.exp(g_last) * states[h] + lax.dot_general(
            kdec.astype(bf16), vn16[h], _TN, preferred_element_type=jnp.float32)
        on = o[h] * lax.rsqrt(jnp.mean(o[h] * o[h], axis=-1, keepdims=True) + EPS) * ng_ref[...]
        o_ref[:, hsl[h]] = (on * _silu(gate[:, hsl[h]])).astype(o_ref.dtype)

    @pl.when(ci == n_chunks - 1)
    def _():
        sout_ref[...] = s_ref[...]


def _gdn_prompt(z2d, zt2d, batch, seq, conv_w, a_log, dt_bias, norm_g):
    L = CHUNK
    nc = seq // L
    gw = GROUP_WIDTH
    ab = jnp.zeros((2, 128), jnp.float32)
    ab = ab.at[0, H_D:2 * H_D].set(a_log.astype(jnp.float32)).at[1, H_D:2 * H_D].set(dt_bias.astype(jnp.float32))
    zspec = lambda blk: pl.BlockSpec((L, gw), lambda b, ci: (b * nc + ci, blk))
    const = lambda shape: pl.BlockSpec(shape, lambda b, ci: (0,) * len(shape))
    return pl.pallas_call(
        functools.partial(_gdn_body, n_chunks=nc),
        grid=(batch, nc),
        in_specs=[zspec(_DQ_BLK), zspec(_DQ_BLK + 1), zspec(_DQ_BLK + 2), zspec(_DG_BLK),
                  pl.BlockSpec((L, 128), lambda b, ci: (b * nc + ci, 0)),
                  const((CONV_W, W_D_QKV)), const((2, 128)), const((1, DV_D))],
        out_specs=[pl.BlockSpec((L, gw), lambda b, ci: (b * nc + ci, 0)),
                   pl.BlockSpec((None, H_D, DK_D, DV_D), lambda b, ci: (b, 0, 0, 0))],
        out_shape=[jax.ShapeDtypeStruct((batch * seq, gw), jnp.bfloat16),
                   jax.ShapeDtypeStruct((batch, H_D, DK_D, DV_D), jnp.float32)],
        scratch_shapes=[pltpu.VMEM((H_D, DK_D, DV_D), jnp.float32),
                        pltpu.VMEM((CONV_CARRY, W_D_QKV), jnp.float32)],
        compiler_params=pltpu.CompilerParams(dimension_semantics=("parallel", "arbitrary")),
    )(z2d, z2d, z2d, z2d, zt2d, conv_w.astype(jnp.float32), ab, norm_g.reshape(1, DV_D).astype(jnp.float32))


_BQ_BLK = W_A // GROUP_WIDTH
_STEP_VECS = 8


def _logaddexp(a, b):
    return jnp.maximum(a, b) + jnp.log(1.0 + jnp.exp(-jnp.abs(a - b)))


def _log_sigmoid(x):
    return jnp.minimum(x, 0.0) - jnp.log(1.0 + jnp.exp(-jnp.abs(x)))


def _gla_body(bq_ref, bf_ref, bi_ref, bg_ref, lb_ref, ng_ref, o_ref, sout_ref, st_ref, *, n_chunks):
    bf16 = jnp.bfloat16
    f32 = jnp.float32
    L = CHUNK
    ci = pl.program_id(1)

    @pl.when(ci == 0)
    def _():
        st_ref[...] = jnp.zeros_like(st_ref)

    levels = []
    m = L // 2
    while m >= 1:
        levels.append(m)
        m //= 2
    nl = len(levels)
    rr = lax.broadcasted_iota(jnp.int32, (nl * L, L), 0)
    cc = lax.broadcasted_iota(jnp.int32, (nl * L, L), 1)
    tt = rr % L
    sel = jnp.zeros((nl * L, L), f32)
    for i, m in enumerate(levels):
        sel = jnp.where((rr // L == i) & (cc == (tt // (2 * m)) * (2 * m) + m - 1), 1.0, sel)
    sel = sel.astype(bf16)
    r = lax.broadcasted_iota(jnp.int32, (L, L), 0)
    c = lax.broadcasted_iota(jnp.int32, (L, L), 1)
    row = lax.broadcasted_iota(jnp.int32, (L, 1), 0)

    heads = range(H_B)
    hsl = [slice(h * DK_B, (h + 1) * DK_B) for h in heads]
    bq, fl_all, bi, bg = bq_ref[...], bf_ref[...], bi_ref[...], bg_ref[...]
    logf_all = _logaddexp(lb_ref[0:1, :], lb_ref[1:2, :] + _log_sigmoid(fl_all))
    b_all = _cumsum_rows(logf_all)
    q_all = _silu(bq) * (DK_B ** -0.5)
    k_all = lb_ref[2:3, :] * jax.nn.sigmoid(-fl_all)

    states = [st_ref[h] for h in heads]
    q = [q_all[:, hsl[h]] for h in heads]
    k = [k_all[:, hsl[h]] for h in heads]
    b = [b_all[:, hsl[h]] for h in heads]
    v16 = [bi[:, hsl[h]].astype(bf16) for h in heads]

    def pieces(x):
        p1 = x.astype(bf16)
        r1 = x - p1.astype(f32)
        p2 = r1.astype(bf16)
        p3 = (r1 - p2.astype(f32)).astype(bf16)
        return p1, p2, p3

    def pick(h):
        p1, p2, p3 = pieces(b[h])
        d = lambda p: jnp.dot(sel, p, preferred_element_type=f32)
        return (d(p1) + d(p2)) + d(p3)

    refs = [pick(h) for h in heads]
    att = [jnp.where(r == c, jnp.sum(q[h] * k[h], axis=-1, keepdims=True), 0.0) for h in heads]
    for i, m in enumerate(levels):
        upper = (row // m) % 2 == 1
        quad = (r // (2 * m) == c // (2 * m)) & ((r // m) % 2 == 1) & ((c // m) % 2 == 0)
        for h in heads:
            ref = refs[h][i * L:(i + 1) * L, :]
            qf = jnp.where(upper, q[h] * jnp.exp(jnp.minimum(b[h] - ref, 0.0)), 0.0)
            kf = jnp.where(upper, 0.0, k[h] * jnp.exp(jnp.minimum(ref - b[h], 0.0)))
            pm = lax.dot_general(qf.astype(bf16), kf.astype(bf16), _NT, preferred_element_type=f32)
            att[h] = att[h] + jnp.where(quad, pm, 0.0)
    s16 = [states[h].astype(bf16) for h in heads]
    o = [lax.dot_general((q[h] * jnp.exp(b[h])).astype(bf16), s16[h], _NT, preferred_element_type=f32)
         + jnp.dot(att[h].astype(bf16), v16[h], preferred_element_type=f32) for h in heads]
    for h in heads:
        b_last = b[h][L - 1:L, :]
        kdec = k[h] * jnp.exp(b_last - b[h])
        st_ref[h] = jnp.exp(b_last) * states[h] + lax.dot_general(v16[h], kdec.astype(bf16), _TN,
                                                                  preferred_element_type=f32)
        on = o[h] * lax.rsqrt(jnp.mean(o[h] * o[h], axis=-1, keepdims=True) + EPS) * ng_ref[...]
        o_ref[:, hsl[h]] = (on * _silu(bg[:, hsl[h]])).astype(o_ref.dtype)

    @pl.when(ci == n_chunks - 1)
    def _():
        for h in heads:
            sout_ref[h] = st_ref[h].T


def _gla_prompt(z2d, batch, seq, lb, norm_g):
    f32 = jnp.float32
    L = CHUNK
    nc = seq // L
    gw = GROUP_WIDTH
    lbf = lb.astype(f32).reshape(1, gw)
    lb_rows = jnp.concatenate([jnp.log(lbf), jnp.log1p(-lbf), 1.0 - lbf], axis=0)
    zspec = lambda blk: pl.BlockSpec((L, gw), lambda b, ci: (b * nc + ci, blk))
    const = lambda shape: pl.BlockSpec(shape, lambda b, ci: (0,) * len(shape))
    return pl.pallas_call(
        functools.partial(_gla_body, n_chunks=nc),
        grid=(batch, nc),
        in_specs=[zspec(_BQ_BLK), zspec(_BQ_BLK + 1), zspec(_BQ_BLK + 2), zspec(_BQ_BLK + 3),
                  const((3, gw)), const((1, DV_B))],
        out_specs=[pl.BlockSpec((L, gw), lambda b, ci: (b * nc + ci, 0)),
                   pl.BlockSpec((None, H_B, DK_B, DV_B), lambda b, ci: (b, 0, 0, 0))],
        out_shape=[jax.ShapeDtypeStruct((batch * seq, gw), jnp.bfloat16),
                   jax.ShapeDtypeStruct((batch, H_B, DK_B, DV_B), f32)],
        scratch_shapes=[pltpu.VMEM((H_B, DV_B, DK_B), f32)],
        compiler_params=pltpu.CompilerParams(dimension_semantics=("parallel", "arbitrary")),
    )(z2d, z2d, z2d, z2d, lb_rows, norm_g.reshape(1, DV_B).astype(f32))


def _rec_step_body(bq_ref, bf_ref, bi_ref, bg_ref, xq_ref, xk_ref, xv_ref, dg_ref, tail_ref, buf_ref,
                   sb_ref, sd_ref, lb_ref, cw_ref, ab_ref, nb_ref, nd_ref,
                   ob_ref, od_ref, sbo_ref, sdo_ref):
    gw = GROUP_WIDTH
    q_b = _silu(bq_ref[...]) * (DK_B ** -0.5)
    fl = bf_ref[...]
    dec_b = jnp.exp(_logaddexp(lb_ref[0:1, :], lb_ref[1:2, :] + _log_sigmoid(fl)))
    k_b = lb_ref[2:3, :] * jax.nn.sigmoid(-fl)
    v_b = bi_ref[...]
    buf = buf_ref[...]

    def conv(x_ref, part):
        cols = slice(part * gw, (part + 1) * gw)
        y = x_ref[...] * cw_ref[CONV_W - 1:CONV_W, cols]
        for j in range(CONV_W - 1):
            y = y + buf[j:j + 1, cols] * cw_ref[j:j + 1, cols]
        return _silu(y)

    qd, kd, vd = conv(xq_ref, 0), conv(xk_ref, 1), conv(xv_ref, 2)
    tail = tail_ref[...]
    beta_all = jax.nn.sigmoid(tail)
    eg_all = jnp.exp(-jnp.exp(ab_ref[0:1, :]) * _softplus(tail + ab_ref[1:2, :]))

    rows = []
    for h in range(H_B):
        hs = slice(h * DK_B, (h + 1) * DK_B)
        rows += [dec_b[:, hs], k_b[:, hs], q_b[:, hs],
                 _l2n(kd[:, hs]), _l2n(qd[:, hs]) * (DK_D ** -0.5),
                 jnp.zeros((_STEP_VECS - 5, DK_B), jnp.float32)]
    rows.append(jnp.zeros((128 - H_B * _STEP_VECS, DK_B), jnp.float32))
    cols_t = jnp.concatenate(rows, axis=0).T

    for h in range(H_B):
        hs = slice(h * DK_B, (h + 1) * DK_B)
        c0 = h * _STEP_VECS
        dec_c, k_c, q_c = cols_t[:, c0:c0 + 1], cols_t[:, c0 + 1:c0 + 2], cols_t[:, c0 + 2:c0 + 3]
        kd_c, qd_c = cols_t[:, c0 + 3:c0 + 4], cols_t[:, c0 + 4:c0 + 5]
        s_new = dec_c * sb_ref[h] + k_c * v_b[:, hs]
        sbo_ref[h] = s_new
        o = jnp.sum(q_c * s_new, axis=0, keepdims=True)
        ob_ref[:, hs] = _head_norm(o, nb_ref[...], 1.0) * _silu(bg_ref[:, hs])
        s = sd_ref[h]
        eg = eg_all[:, H_D + h:H_D + h + 1]
        beta = beta_all[:, h:h + 1]
        ks = jnp.sum(kd_c * s, axis=0, keepdims=True)
        v_new = beta * (vd[:, hs] - eg * ks)
        s_new = eg * s + kd_c * v_new
        sdo_ref[h] = s_new
        o = jnp.sum(qd_c * s_new, axis=0, keepdims=True)
        od_ref[:, hs] = _head_norm(o, nd_ref[...], 1.0) * _silu(dg_ref[:, hs])


def _rec_step(z2d, zt2d, conv_buf, s_b, s_d, lb, conv_w, a_log, dt_bias, norm_b, norm_d):
    f32 = jnp.float32
    batch = z2d.shape[0]
    gw = GROUP_WIDTH
    z3 = z2d.reshape(batch, 1, N_IN_MAIN)
    zt3 = zt2d.reshape(batch, 1, 128)
    lbf = lb.astype(f32).reshape(1, gw)
    lb_rows = jnp.concatenate([jnp.log(lbf), jnp.log1p(-lbf), 1.0 - lbf], axis=0)
    ab = jnp.zeros((2, 128), f32)
    ab = ab.at[0, H_D:2 * H_D].set(a_log.astype(f32)).at[1, H_D:2 * H_D].set(dt_bias.astype(f32))
    zspec = lambda blk: pl.BlockSpec((None, 1, gw), lambda b: (b, 0, blk))
    const = lambda shape: pl.BlockSpec(shape, lambda b: (0,) * len(shape))
    st_spec = pl.BlockSpec((None, H_B, DK_B, DV_B), lambda b: (b, 0, 0, 0))
    o_spec = pl.BlockSpec((None, 1, gw), lambda b: (b, 0, 0))
    ob, od, sbo, sdo = pl.pallas_call(
        _rec_step_body,
        grid=(batch,),
        in_specs=[zspec(_BQ_BLK), zspec(_BQ_BLK + 1), zspec(_BQ_BLK + 2), zspec(_BQ_BLK + 3),
                  zspec(_DQ_BLK), zspec(_DQ_BLK + 1), zspec(_DQ_BLK + 2), zspec(_DG_BLK),
                  pl.BlockSpec((None, 1, 128), lambda b: (b, 0, 0)),
                  pl.BlockSpec((None, CONV_W - 1, W_D_QKV), lambda b: (b, 0, 0)),
                  st_spec, st_spec,
                  const((3, gw)), const((CONV_W, W_D_QKV)), const((2, 128)),
                  const((1, DV_B)), const((1, DV_D))],
        out_specs=[o_spec, o_spec, st_spec, st_spec],
        out_shape=[jax.ShapeDtypeStruct((batch, 1, gw), f32), jax.ShapeDtypeStruct((batch, 1, gw), f32),
                   jax.ShapeDtypeStruct(s_b.shape, f32), jax.ShapeDtypeStruct(s_d.shape, f32)],
        compiler_params=pltpu.CompilerParams(dimension_semantics=("parallel",)),
    )(z3, z3, z3, z3, z3, z3, z3, z3, zt3, conv_buf.astype(f32), s_b.astype(f32), s_d.astype(f32),
      lb_rows, conv_w.astype(f32), ab, norm_b.reshape(1, DV_B).astype(f32), norm_d.reshape(1, DV_D).astype(f32))
    return ob, od, sbo, sdo


def _rms_norm(x, g):
    xf = x.astype(jnp.float32)
    y = xf * lax.rsqrt(jnp.mean(xf * xf, axis=-1, keepdims=True) + EPS)
    return (y * g.astype(jnp.float32)).astype(x.dtype)


def _l2norm(x):
    return x * lax.rsqrt(jnp.sum(x * x, axis=-1, keepdims=True) + EPS)


def _complex_affine_combine(e1, e2):
    a1r, a1i, b1r, b1i = e1
    a2r, a2i, b2r, b2i = e2
    return (a2r * a1r - a2i * a1i, a2r * a1i + a2i * a1r,
            a2r * b1r - a2i * b1i + b2r, a2r * b1i + a2i * b1r + b2i)


def _s5(u, lam_re, lam_im, log_dt, b_re, b_im, c_re, c_im, d_skip, h0_re, h0_im):
    f32 = jnp.float32
    B, T, _ = u.shape
    uf = u.astype(f32).reshape(B, T, G_A, SSM_CH)
    lr, li = lam_re.astype(f32), lam_im.astype(f32)
    dt = jnp.exp(log_dt.astype(f32))[:, None]
    mag = jnp.exp(lr * dt)
    ab_re, ab_im = mag * jnp.cos(li * dt), mag * jnp.sin(li * dt)
    den = lr * lr + li * li
    fr = ((ab_re - 1.0) * lr + ab_im * li) / den
    fi = (ab_im * lr - (ab_re - 1.0) * li) / den
    br, bi = b_re.astype(f32), b_im.astype(f32)
    bb_re = fr[..., None] * br - fi[..., None] * bi
    bb_im = fr[..., None] * bi + fi[..., None] * br
    bu_re = jnp.einsum('btgc,gpc->btgp', uf, bb_re)
    bu_im = jnp.einsum('btgc,gpc->btgp', uf, bb_im)
    a_re = jnp.broadcast_to(ab_re, bu_re.shape)
    a_im = jnp.broadcast_to(ab_im, bu_im.shape)
    acc_re, acc_im, s_re, s_im = lax.associative_scan(
        _complex_affine_combine, (a_re, a_im, bu_re, bu_im), axis=1)
    h0r = h0_re.astype(f32)[:, None]
    h0i = h0_im.astype(f32)[:, None]
    hr = s_re + acc_re * h0r - acc_im * h0i
    hi = s_im + acc_re * h0i + acc_im * h0r
    y = (jnp.einsum('gcp,btgp->btgc', c_re.astype(f32), hr)
         - jnp.einsum('gcp,btgp->btgc', c_im.astype(f32), hi)
         + d_skip.astype(f32) * uf).reshape(B, T, W_A)
    return jax.nn.gelu(y), hr[:, -1], hi[:, -1]


def _to_chunks(a, L, n):
    B, T = a.shape[:2]
    a = jnp.pad(a, [(0, 0), (0, n * L - T)] + [(0, 0)] * (a.ndim - 2))
    a = a.reshape((B, n, L) + a.shape[2:])
    a = jnp.moveaxis(a, 1, 0)
    return jnp.swapaxes(a, 2, 3)


def _from_chunks(o, T):
    n, B, H, L = o.shape[:4]
    o = jnp.moveaxis(jnp.swapaxes(o, 2, 3), 0, 1)
    return o.reshape((B, n * L) + o.shape[3:])[:, :T]


def _gla_chunked(q, k, v, logf, S0):
    T = q.shape[1]
    L = min(CHUNK, T)
    n = -(-T // L)
    qc, kc, vc, gc = (_to_chunks(a, L, n) for a in (q, k, v, logf))
    tri = jnp.tril(jnp.ones((L, L), dtype=bool))

    def step(S, inp):
        qi, ki, vi, gi = inp
        b = jnp.cumsum(gi, axis=2)
        dec = jnp.exp(jnp.where(tri[:, :, None], b[:, :, :, None, :] - b[:, :, None, :, :], -jnp.inf))
        att = jnp.sum(qi[:, :, :, None, :] * ki[:, :, None, :, :] * dec, axis=-1)
        o = (jnp.einsum('bhtk,bhkv->bhtv', qi * jnp.exp(b), S)
             + jnp.einsum('bhts,bhsv->bhtv', att, vi))
        b_last = b[:, :, -1:, :]
        S_new = (jnp.exp(b_last[:, :, 0, :])[..., None] * S
                 + jnp.einsum('bhlk,bhlv->bhkv', ki * jnp.exp(b_last - b), vi))
        return S_new, o

    S, o = lax.scan(step, S0, (qc, kc, vc, gc))
    return _from_chunks(o, T), S


def _gdn_chunked(q, k, v, beta, logg, S0):
    T = q.shape[1]
    V = v.shape[-1]
    L = min(CHUNK, T)
    n = -(-T // L)
    qc, kc, vc, bc, gc = (_to_chunks(a, L, n) for a in (q, k, v, beta, logg))
    tri = jnp.tril(jnp.ones((L, L), dtype=bool))
    strict = jnp.tril(jnp.ones((L, L), dtype=bool), -1)
    eye = jnp.eye(L, dtype=jnp.float32)

    def step(S, inp):
        qi, ki, vi, bi, gi = inp
        gcum = jnp.cumsum(gi, axis=-1)
        ldec = jnp.exp(jnp.where(tri, gcum[..., :, None] - gcum[..., None, :], -jnp.inf))
        kb = ki * bi[..., None]
        m = jnp.where(strict, jnp.einsum('bhtk,bhsk->bhts', kb, ki) * ldec, 0.0)
        rhs = jnp.concatenate([vi * bi[..., None], kb * jnp.exp(gcum)[..., None]], axis=-1)
        sol = lax.linalg.triangular_solve(eye + m, rhs, left_side=True, lower=True, unit_diagonal=True)
        u, w = sol[..., :V], sol[..., V:]
        v_new = u - jnp.einsum('bhlk,bhkv->bhlv', w, S)
        att = jnp.einsum('bhtk,bhsk->bhts', qi, ki) * ldec
        o = (jnp.einsum('bhlk,bhkv->bhlv', qi * jnp.exp(gcum)[..., None], S)
             + jnp.einsum('bhts,bhsv->bhtv', att, v_new))
        g_last = gcum[..., -1]
        S_new = (jnp.exp(g_last)[..., None, None] * S
                 + jnp.einsum('bhlk,bhlv->bhkv', ki * jnp.exp(g_last[..., None] - gcum)[..., None], v_new))
        return S_new, o

    S, o = lax.scan(step, S0, (qc, kc, vc, bc, gc))
    return _from_chunks(o, T), S


def _causal_conv(x, buf, w):
    T = x.shape[1]
    xx = jnp.concatenate([buf.astype(x.dtype), x], axis=1)
    y = xx[:, 0:T] * w[0]
    for j in range(1, CONV_W):
        y = y + xx[:, j:j + T] * w[j]
    return jax.nn.silu(y), xx[:, T:]


def _t5_bucket(qpos, kpos):
    n = jnp.maximum(qpos[:, None] - kpos[None, :], 0)
    max_exact = N_BUCKETS // 2
    nf = jnp.maximum(n, 1).astype(jnp.float32)
    large = max_exact + (jnp.log(nf / max_exact) / math.log(MAX_DISTANCE / max_exact)
                         * (N_BUCKETS - max_exact)).astype(jnp.int32)
    large = jnp.minimum(large, N_BUCKETS - 1)
    return jnp.where(n < max_exact, n, large)


def _diff_attn_core(q, k, v, qpos, kpos, lam, rel_table):
    s = jnp.einsum('bqhcd,bkhcd->bchqk', q, k) * (DH_C ** -0.5)
    bias = jnp.transpose(rel_table.astype(jnp.float32)[_t5_bucket(qpos, kpos)], (2, 0, 1))
    causal = kpos[None, :] <= qpos[:, None]
    s = jnp.where(causal, s + bias, -jnp.inf)
    p = jax.nn.softmax(s, axis=-1)
    a = p[:, 0] - lam * p[:, 1]
    return jnp.einsum('bhqk,bkhe->bqhe', a, v)


def _diff_attn_prompt(q, k, v, lam, rel_table):
    B, T = q.shape[:2]
    qb = min(Q_BLOCK, T)
    nb = T // qb
    q_blocks = jnp.moveaxis(q.reshape((B, nb, qb) + q.shape[2:]), 1, 0)
    qpos = jnp.arange(T, dtype=jnp.int32).reshape(nb, qb)
    kpos = jnp.arange(T, dtype=jnp.int32)
    out = lax.map(lambda blk: _diff_attn_core(blk[0], k, v, blk[1], kpos, lam, rel_table), (q_blocks, qpos))
    return jnp.moveaxis(out, 0, 1).reshape((B, T) + out.shape[3:])


def _split_main():
    sizes = [W_A, H_B * DK_B, H_B * DK_B, H_B * DV_B, H_B * DV_B,
             2 * H_C * DH_C, 2 * H_C * DH_C, 2 * H_C * DH_C, W_D_QKV, H_D * DV_D]
    pts, acc = [], 0
    for s in sizes[:-1]:
        acc += s
        pts.append(acc)
    return pts


def _layer(l, x, mod, lb, p, past):
    f32 = jnp.float32
    bf16 = jnp.bfloat16
    B, T, _ = x.shape
    M = B * T
    sh1, sc1, gt1, sh2, sc2, gt2 = jnp.split(mod[:, None, :], 6, axis=-1)
    Mp = max(M, 16)
    tm, tn, tk = _tiles(Mp)

    def pad_rows(a):
        return a if Mp == M else jnp.pad(a, ((0, Mp - M), (0, 0)))

    h = _rms_norm(x, p['g_norm1'][l]) * (1.0 + sc1) + sh1
    h2d = pad_rows(h.reshape(M, D_MODEL).astype(bf16))
    z = _matmul(h2d, p['w_in'], l, N_IN_MAIN, tm=tm, tn=tn, tk=tk, out_dtype=f32)
    zt = _matmul(h2d, p['w_in_tail'], l, 128, tm=tm, tn=128, tk=tk, out_dtype=f32)
    z2d = z[:M]
    z = z2d.reshape(B, T, N_IN_MAIN)
    zt = zt[:M].reshape(B, T, 128)
    (a_u, b_q, b_f, b_i, b_g, c_q, c_k, c_v, d_qkv, d_g) = jnp.split(z, _split_main(), axis=-1)
    d_beta, d_a = zt[..., :H_D], zt[..., H_D:2 * H_D]

    if past is None:
        h0_re = jnp.zeros((B, G_A, P_A), f32)
        h0_im = jnp.zeros((B, G_A, P_A), f32)
        s0_b = jnp.zeros((B, H_B, DK_B, DV_B), f32)
        s0_d = jnp.zeros((B, H_D, DK_D, DV_D), f32)
        conv0 = jnp.zeros((B, CONV_W - 1, W_D_QKV), z.dtype)
    else:
        h0_re, h0_im = past['ssm_re'][l], past['ssm_im'][l]
        s0_b, s0_d, conv0 = past['hgrn'][l], past['gdn'][l], past['conv'][l]

    s5_params = (p['ssm_lam_re'][l], p['ssm_lam_im'][l], p['ssm_log_dt'][l], p['ssm_b_re'][l],
                 p['ssm_b_im'][l], p['ssm_c_re'][l], p['ssm_c_im'][l], p['ssm_d'][l])
    if past is None:
        y_a, hr, hi = _s5_prompt(a_u, _s5_tables(*s5_params, T // S5_L))
    else:
        y_a, hr, hi = _s5_step(a_u, h0_re, h0_im, _s5_tables(*s5_params, 1, L=1))
    o_a = _s5_glu(pad_rows(y_a.reshape(M, W_A)), p['ssm_w_glu'], p['ssm_b_glu'], l,
                  min(Mp, 512))[:M].reshape(B, T, W_A)

    if past is None:
        o_b, s_b = _gla_prompt(z2d, B, T, lb, p['hgrn_norm'][l])
    else:
        o_b, o_d, s_b, s_d = _rec_step(z2d, zt[:, 0, :], conv0, s0_b, s0_d, lb, p['gdn_conv'][l],
                                       p['gdn_a_log'][l], p['gdn_dt_bias'][l], p['hgrn_norm'][l],
                                       p['gdn_norm'][l])

    lam_init = 0.8 - 0.6 * math.exp(-0.3 * l)
    lam = (jnp.exp(jnp.sum(p['diff_lq1'][l].astype(f32) * p['diff_lk1'][l].astype(f32)))
           - jnp.exp(jnp.sum(p['diff_lq2'][l].astype(f32) * p['diff_lk2'][l].astype(f32))) + lam_init)
    kh = c_k.reshape(B, T, H_C, 2 * DH_C)
    vh = c_v.reshape(B, T, H_C, 2 * DH_C)
    if past is None:
        o_c = _diff_attn_prompt_pallas(z2d, B, T, lam, _attn_bias_tiles(p['rel_bias']),
                                       p['diff_norm'][l], 1.0 - lam_init)
    else:
        o_c = _diff_attn_step_pallas(c_q.reshape(M, -1), c_k.reshape(M, -1), c_v.reshape(M, -1),
                                     past['cache_k'], past['cache_v'], past['page_table'], l, lam,
                                     p['rel_bias'], p['diff_norm'][l], 1.0 - lam_init)
    o_c = o_c.reshape(B, T, 2 * H_C * DH_C)

    conv_new = jnp.concatenate([conv0.astype(d_qkv.dtype), d_qkv], axis=1)[:, T:]
    if past is None:
        o_d, s_d = _gdn_prompt(z2d, zt.reshape(M, 128), B, T, p['gdn_conv'][l], p['gdn_a_log'][l],
                               p['gdn_dt_bias'][l], p['gdn_norm'][l])
    o_b = o_b.reshape(B, T, H_B * DV_B)
    o_d = o_d.reshape(B, T, H_D * DV_D)

    o = pad_rows(jnp.concatenate([o_a, o_b, o_c, o_d], axis=-1).reshape(M, D_MODEL).astype(bf16))
    x2d = pad_rows(x.reshape(M, D_MODEL))

    def gate_rows(g):
        if T < tm:
            return pad_rows(jnp.broadcast_to(g, (B, T, D_MODEL)).reshape(M, D_MODEL)), \
                pl.BlockSpec((tm, tn), lambda i, j, k: (i, j))
        per = T // tm
        return g.reshape(B, 1, D_MODEL), \
            pl.BlockSpec((None, 1, tn), lambda i, j, k: (i // per, 0, j))

    g1, g1_spec = gate_rows(gt1)
    x2d = _matmul(o, p['w_out'], l, D_MODEL, tm=tm, tn=tn, tk=tk, out_dtype=f32,
                  epilogue="gated_residual", extras=(x2d, g1),
                  extra_specs=(pl.BlockSpec((tm, tn), lambda i, j, k: (i, j)), g1_spec))

    xm = x2d[:M].reshape(B, T, D_MODEL)
    h2 = pad_rows((_rms_norm(xm, p['g_norm2'][l]) * (1.0 + sc2) + sh2).reshape(M, D_MODEL).astype(bf16))
    ff = _matmul(h2, p['w_up'], l, D_FF, tm=tm, tn=tn, tk=tk, out_dtype=bf16, epilogue="relu2")
    g2, g2_spec = gate_rows(gt2)
    x2d = _matmul(ff, p['w_down'], l, D_MODEL, tm=tm, tn=tn, tk=tk, out_dtype=f32,
                  epilogue="gated_residual", extras=(x2d, g2),
                  extra_specs=(pl.BlockSpec((tm, tn), lambda i, j, k: (i, j)), g2_spec))
    return x2d[:M].reshape(B, T, D_MODEL), (kh, vh, hr, hi, s_b, s_d, conv_new)


def _trunk(x, mods, p, past):
    sm = jax.nn.softmax(p['hgrn_lb_logits'].astype(jnp.float32), axis=0)
    lb_all = jnp.cumsum(sm, axis=0)
    lb_all = lb_all - lb_all[0:1]
    cols = [[] for _ in range(7)]
    for l in range(DEPTH):
        x, st = _layer(l, x, mods[l], lb_all[l], p, past)
        for j in range(7):
            cols[j].append(st[j])
    y = _rms_norm(x, p['g_final'])
    return y, [jnp.stack(col, axis=0) for col in cols]


def kernel(x_prompt, x_sample, c_prompt, c_sample, cache_k, cache_v, page_table, state_ssm_re, state_ssm_im, state_hgrn, state_gdn, state_gdn_conv, w_ada, b_ada, g_norm1, w_in, ssm_lam_re, ssm_lam_im, ssm_log_dt, ssm_b_re, ssm_b_im, ssm_c_re, ssm_c_im, ssm_d, ssm_w_glu, ssm_b_glu, hgrn_lb_logits, hgrn_norm, diff_lq1, diff_lk1, diff_lq2, diff_lk2, diff_norm, rel_bias, gdn_conv, gdn_a_log, gdn_dt_bias, gdn_norm, w_out, g_norm2, w_up, w_down, g_final):
    w_in_tail = jnp.pad(w_in[:, :, N_IN_MAIN:], ((0, 0), (0, 0), (0, 128 - N_TAIL)))
    p = dict(w_ada=w_ada, b_ada=b_ada, g_norm1=g_norm1, w_in=w_in, w_in_tail=w_in_tail,
             ssm_lam_re=ssm_lam_re, ssm_lam_im=ssm_lam_im, ssm_log_dt=ssm_log_dt,
             ssm_b_re=ssm_b_re, ssm_b_im=ssm_b_im, ssm_c_re=ssm_c_re, ssm_c_im=ssm_c_im,
             ssm_d=ssm_d, ssm_w_glu=ssm_w_glu, ssm_b_glu=ssm_b_glu,
             hgrn_lb_logits=hgrn_lb_logits, hgrn_norm=hgrn_norm,
             diff_lq1=diff_lq1, diff_lk1=diff_lk1, diff_lq2=diff_lq2, diff_lk2=diff_lk2,
             diff_norm=diff_norm, rel_bias=rel_bias,
             gdn_conv=gdn_conv, gdn_a_log=gdn_a_log, gdn_dt_bias=gdn_dt_bias, gdn_norm=gdn_norm,
             w_out=w_out, g_norm2=g_norm2, w_up=w_up, w_down=w_down, g_final=g_final)

    nb_p, nb_s = c_prompt.shape[0], c_sample.shape[0]
    c_all = jnp.concatenate([c_prompt, c_sample], axis=0)
    rows = 16
    cs = jnp.pad(jax.nn.silu(c_all), ((0, rows - nb_p - nb_s), (0, 0))).astype(jnp.bfloat16)
    mods_p, mods_s = [], []
    for l in range(DEPTH):
        m = _matmul(cs, w_ada, l, 6 * D_MODEL, tm=rows, tn=2048, tk=1024, out_dtype=jnp.float32,
                    epilogue="bias", extras=(b_ada[l][None, :],),
                    extra_specs=(pl.BlockSpec((1, 2048), lambda i, j, k: (0, j)),))
        mods_p.append(m[:nb_p])
        mods_s.append(m[nb_p:nb_p + nb_s])

    y_prompt, sp = _trunk(x_prompt, mods_p, p, None)

    past = dict(cache_k=cache_k, cache_v=cache_v, page_table=page_table,
                ssm_re=state_ssm_re, ssm_im=state_ssm_im, hgrn=state_hgrn,
                gdn=state_gdn, conv=state_gdn_conv)
    y_sample, ss = _trunk(x_sample, mods_s, p, past)
    return (y_prompt, y_sample,
            sp[0], sp[1], sp[2], sp[3], sp[4], sp[5], sp[6],
            ss[0], ss[1], ss[2], ss[3], ss[4], ss[5], ss[6])
```

```python
import functools
import math

import jax
import jax.numpy as jnp
from jax import lax
from jax.experimental import pallas as pl
from jax.experimental.pallas import tpu as pltpu

D_MODEL = 4096
DEPTH = 4
PAGE_SIZE = 128
GROUP_WIDTH = D_MODEL // 4
SSM_CH = 16
G_A = GROUP_WIDTH // SSM_CH
P_A = 64
W_A = G_A * SSM_CH
H_B = 8
DK_B = 128
DV_B = GROUP_WIDTH // H_B
H_C = 4
DH_C = GROUP_WIDTH // (2 * H_C)
N_BUCKETS = 32
MAX_DISTANCE = 128
Q_BLOCK = 128
H_D = 8
DK_D = 128
DV_D = GROUP_WIDTH // H_D
CONV_W = 4
W_D_QKV = H_D * (2 * DK_D + DV_D)
CHUNK = 64
D_FF = 4 * D_MODEL
N_IN = W_A + 2 * H_B * DK_B + 2 * H_B * DV_B + 6 * H_C * DH_C + W_D_QKV + H_D * DV_D + 2 * H_D
N_IN_MAIN = (N_IN // 128) * 128
N_TAIL = N_IN - N_IN_MAIN
EPS = 1e-6

VMEM_LIMIT_BYTES = 56 * 1024 * 1024


def _mm_epilogue(acc, extra_refs, epilogue):
    if epilogue == "none":
        return acc
    if epilogue == "bias":
        return acc + extra_refs[0][...]
    if epilogue == "relu2":
        r = jnp.maximum(acc, 0.0)
        return r * r
    if epilogue == "gated_residual":
        return extra_refs[0][...] + extra_refs[1][...] * acc
    raise ValueError(epilogue)


def _mm_body(a_ref, w_ref, *rest, nk, epilogue):
    prod = jnp.dot(a_ref[...], w_ref[...].astype(jnp.bfloat16), preferred_element_type=jnp.float32)
    if nk == 1:
        *extra_refs, o_ref = rest
        o_ref[...] = _mm_epilogue(prod, extra_refs, epilogue).astype(o_ref.dtype)
        return
    *extra_refs, o_ref, acc_ref = rest
    k = pl.program_id(2)

    @pl.when(k == 0)
    def _():
        acc_ref[...] = prod

    @pl.when((k > 0) & (k < nk - 1))
    def _():
        acc_ref[...] += prod

    @pl.when(k == nk - 1)
    def _():
        o_ref[...] = _mm_epilogue(acc_ref[...] + prod, extra_refs, epilogue).astype(o_ref.dtype)


def _matmul(a, w, layer, n_cols, *, tm, tn, tk, out_dtype, epilogue="none",
            extras=(), extra_specs=()):
    m, k_dim = a.shape
    assert w.shape[1] == k_dim and m % tm == 0 and n_cols % tn == 0 and k_dim % tk == 0
    nk = k_dim // tk
    grid = (m // tm, n_cols // tn, nk)
    in_specs = [
        pl.BlockSpec((tm, tk), lambda i, j, k: (i, k)),
        pl.BlockSpec((None, tk, tn), lambda i, j, k: (layer, k, j)),
    ] + list(extra_specs)
    return pl.pallas_call(
        functools.partial(_mm_body, nk=nk, epilogue=epilogue),
        grid=grid,
        in_specs=in_specs,
        out_specs=pl.BlockSpec((tm, tn), lambda i, j, k: (i, j)),
        out_shape=jax.ShapeDtypeStruct((m, n_cols), out_dtype),
        scratch_shapes=[pltpu.VMEM((tm, tn), jnp.float32)] if nk > 1 else [],
        compiler_params=pltpu.CompilerParams(
            dimension_semantics=("parallel", "parallel", "arbitrary"),
            vmem_limit_bytes=VMEM_LIMIT_BYTES),
    )(a, w, *extras)


def _tiles(m):
    if m >= 1024:
        return 1024, 512, 4096
    return m, 2048, 1024


def _norm_mod_body(x_ref, g_ref, sc_ref, sh_ref, o_ref):
    x = x_ref[...]
    y = x * lax.rsqrt(jnp.mean(x * x, axis=-1, keepdims=True) + EPS) * g_ref[...]
    o_ref[...] = (y * (1.0 + sc_ref[...]) + sh_ref[...]).astype(o_ref.dtype)


def _norm_mod(x2d, g, scale, shift, rows_per_seq, out_dtype):
    m, d = x2d.shape
    batch = m // rows_per_seq
    tm = min(rows_per_seq, 256)
    per = rows_per_seq // tm
    if rows_per_seq < 8:
        tm, per = m, 1
        mod_spec = pl.BlockSpec((tm, d), lambda i: (0, 0))
        scale, shift = scale.reshape(m, d), shift.reshape(m, d)
    else:
        mod_spec = pl.BlockSpec((None, 1, d), lambda i: (i // per, 0, 0))
        scale, shift = scale.reshape(batch, 1, d), shift.reshape(batch, 1, d)
    return pl.pallas_call(
        _norm_mod_body,
        grid=(m // tm,),
        in_specs=[pl.BlockSpec((tm, d), lambda i: (i, 0)), pl.BlockSpec((1, d), lambda i: (0, 0)),
                  mod_spec, mod_spec],
        out_specs=pl.BlockSpec((tm, d), lambda i: (i, 0)),
        out_shape=jax.ShapeDtypeStruct((m, d), out_dtype),
        compiler_params=pltpu.CompilerParams(dimension_semantics=("parallel",)),
    )(x2d, g.reshape(1, d).astype(jnp.float32), scale, shift)


def _split_bf16(x):
    hi = x.astype(jnp.bfloat16)
    lo = (x - hi.astype(jnp.float32)).astype(jnp.bfloat16)
    return hi, lo


def _dot3(a, b, dims=(((1,), (0,)), ((), ()))):
    ah, al = _split_bf16(a)
    bh, bl = _split_bf16(b)

    def d(x, y):
        return lax.dot_general(x, y, dims, preferred_element_type=jnp.float32)

    return d(ah, bh) + (d(ah, bl) + d(al, bh))


_NT = (((1,), (1,)), ((), ()))
_TN = (((0,), (0,)), ((), ()))


S5_L = 16


def _s5_tables(lam_re, lam_im, log_dt, b_re, b_im, c_re, c_im, d_skip, n_chunks, L=S5_L):
    f32 = jnp.float32
    hp = lax.Precision.HIGHEST
    G, P, C = G_A, P_A, SSM_CH
    lr, li = lam_re.astype(f32), lam_im.astype(f32)
    dt = jnp.exp(log_dt.astype(f32))[:, None]
    mag = jnp.exp(lr * dt)
    ab_re, ab_im = mag * jnp.cos(li * dt), mag * jnp.sin(li * dt)
    den = lr * lr + li * li
    fr = ((ab_re - 1.0) * lr + ab_im * li) / den
    fi = (ab_im * lr - (ab_re - 1.0) * li) / den
    br, bi = b_re.astype(f32), b_im.astype(f32)
    bb_re = fr[..., None] * br - fi[..., None] * bi
    bb_im = fr[..., None] * bi + fi[..., None] * br

    def powers(ns):
        nf = jnp.asarray(ns, f32)[None, :, None]
        m = jnp.exp(nf * (lr * dt)[:, None, :])
        ph = nf * (li * dt)[:, None, :]
        return m * jnp.cos(ph), m * jnp.sin(ph)

    ar, ai = powers(list(range(L + 1)))
    cr, ci = c_re.astype(f32), c_im.astype(f32)
    car = cr[:, None] * ar[:, :, None, :] - ci[:, None] * ai[:, :, None, :]
    cai = cr[:, None] * ai[:, :, None, :] + ci[:, None] * ar[:, :, None, :]
    kk = (jnp.einsum('gncp,gpd->gncd', car[:, :L], bb_re, precision=hp)
          - jnp.einsum('gncp,gpd->gncd', cai[:, :L], bb_im, precision=hp))
    tl = jnp.arange(L)
    lag = tl[:, None] - tl[None, :]
    kfull = kk[:, jnp.clip(lag, 0, L - 1)]
    kfull = jnp.where((lag >= 0)[None, :, :, None, None], kfull, 0.0)
    wt = jnp.transpose(kfull, (0, 2, 4, 1, 3)).reshape(G, L * C, L * C)
    arr, aii = ar[:, L - 1 - tl], ai[:, L - 1 - tl]
    bm_re = arr[:, :, None, :] * jnp.swapaxes(bb_re, 1, 2)[:, None] - aii[:, :, None, :] * jnp.swapaxes(bb_im, 1, 2)[:, None]
    bm_im = arr[:, :, None, :] * jnp.swapaxes(bb_im, 1, 2)[:, None] + aii[:, :, None, :] * jnp.swapaxes(bb_re, 1, 2)[:, None]
    bmt = jnp.concatenate([bm_re, bm_im], axis=-1).reshape(G, L * C, 2 * P)
    cm_re = jnp.transpose(car[:, 1:], (0, 3, 1, 2))
    cm_im = -jnp.transpose(cai[:, 1:], (0, 3, 1, 2))
    cmt = jnp.concatenate([cm_re, cm_im], axis=1).reshape(G, 2 * P, L * C)
    n_steps = max(1, (n_chunks - 1).bit_length())
    sr, si = powers([L * (1 << i) for i in range(n_steps)])
    ca = jnp.concatenate([sr, sr], axis=-1)
    cb = jnp.concatenate([-si, si], axis=-1)
    dtile = jnp.tile(d_skip.astype(f32), (1, L))[:, None, :]
    return wt, bmt, cmt, ca, cb, dtile, n_steps


def _s5_body(x_ref, wt_ref, bm_ref, cm_ref, ca_ref, cb_ref, d_ref, y_ref, st_ref, *,
             n_chunks, n_steps, batch):
    x = x_ref[...]
    rows = x.shape[0]
    e = _dot3(x, bm_ref[...])
    jidx = lax.broadcasted_iota(jnp.int32, (rows, 2 * P_A), 0) % n_chunks
    for i in range(n_steps):
        sh = 1 << i
        xs = jnp.where(jidx >= sh, pltpu.roll(e, sh, axis=0), 0.0)
        e = e + xs * ca_ref[i:i + 1, :] + pltpu.roll(xs, P_A, axis=1) * cb_ref[i:i + 1, :]
    hprev = jnp.where(jidx >= 1, pltpu.roll(e, 1, axis=0), 0.0)
    y_ref[...] = _dot3(x, wt_ref[...]) + _dot3(hprev, cm_ref[...]) + x * d_ref[...]
    for b in range(batch):
        r = b * n_chunks + n_chunks - 1
        st_ref[b:b + 1, :] = e[r:r + 1, :]


def _s5_prompt(a_u, tables):
    wt, bmt, cmt, ca, cb, dtile, n_steps = tables
    B, T, _ = a_u.shape
    L, G, C, P = S5_L, G_A, SSM_CH, P_A
    nc = T // L
    rows = B * nc
    x = a_u.reshape(B, nc, L, G, C).transpose(3, 0, 1, 2, 4).reshape(G, rows, L * C)
    full = lambda *shape: pl.BlockSpec((None,) + shape, lambda g: (g,) + (0,) * len(shape))
    y, st = pl.pallas_call(
        functools.partial(_s5_body, n_chunks=nc, n_steps=n_steps, batch=B),
        grid=(G,),
        in_specs=[full(rows, L * C), full(L * C, L * C), full(L * C, 2 * P), full(2 * P, L * C),
                  full(n_steps, 2 * P), full(n_steps, 2 * P), full(1, L * C)],
        out_specs=[full(rows, L * C), full(B, 2 * P)],
        out_shape=[jax.ShapeDtypeStruct((G, rows, L * C), jnp.float32),
                   jax.ShapeDtypeStruct((G, B, 2 * P), jnp.float32)],
        compiler_params=pltpu.CompilerParams(dimension_semantics=("parallel",)),
    )(x, wt, bmt, cmt, ca, cb, dtile)
    y = y.reshape(G, B, nc, L, C).transpose(1, 2, 3, 0, 4).reshape(B, T, G * C)
    st = jnp.swapaxes(st, 0, 1)
    return y, st[..., :P], st[..., P:]


def _s5_step_body(x_ref, h_ref, w_ref, bm_ref, cm_ref, ca_ref, cb_ref, d_ref, y_ref, st_ref):
    x = x_ref[...]
    h0 = h_ref[...]
    st_ref[...] = (_dot3(x, bm_ref[...]) + h0 * ca_ref[...]
                   + pltpu.roll(h0, P_A, axis=1) * cb_ref[...])
    y_ref[...] = _dot3(x, w_ref[...]) + _dot3(h0, cm_ref[...]) + x * d_ref[...]


def _s5_step(a_u, h0_re, h0_im, tables):
    wt, bmt, cmt, ca, cb, dtile, _ = tables
    B = a_u.shape[0]
    G, C, P = G_A, SSM_CH, P_A
    x = a_u.reshape(B, G, C).transpose(1, 0, 2)
    h0 = jnp.concatenate([h0_re, h0_im], axis=-1).transpose(1, 0, 2)
    full = lambda *shape: pl.BlockSpec((None,) + shape, lambda g: (g,) + (0,) * len(shape))
    y, st = pl.pallas_call(
        _s5_step_body,
        grid=(G,),
        in_specs=[full(B, C), full(B, 2 * P), full(C, C), full(C, 2 * P), full(2 * P, C),
                  full(1, 2 * P), full(1, 2 * P), full(1, C)],
        out_specs=[full(B, C), full(B, 2 * P)],
        out_shape=[jax.ShapeDtypeStruct((G, B, C), jnp.float32),
                   jax.ShapeDtypeStruct((G, B, 2 * P), jnp.float32)],
        compiler_params=pltpu.CompilerParams(dimension_semantics=("parallel",)),
    )(x, h0.astype(jnp.float32), wt, bmt, cmt, ca, cb, dtile)
    y = y.transpose(1, 0, 2).reshape(B, 1, G * C)
    st = jnp.swapaxes(st, 0, 1)
    return y, st[..., :P], st[..., P:]


def _gelu_tanh(x):
    c = math.sqrt(2.0 / math.pi)
    return 0.5 * x * (1.0 + jnp.tanh(c * (x + 0.044715 * (x * x * x))))


def _glu_body(y_ref, w_ref, b_ref, o_ref):
    zg = _gelu_tanh(y_ref[...])
    glu = jnp.dot(zg.astype(jnp.bfloat16), w_ref[...].astype(jnp.bfloat16),
                  preferred_element_type=jnp.float32) + b_ref[...]
    o_ref[...] = (zg * jax.nn.sigmoid(glu)).astype(o_ref.dtype)


def _s5_glu(y2d, w_glu, b_glu, layer, tm):
    m = y2d.shape[0]
    return pl.pallas_call(
        _glu_body,
        grid=(m // tm,),
        in_specs=[pl.BlockSpec((tm, W_A), lambda i: (i, 0)),
                  pl.BlockSpec((None, W_A, W_A), lambda i: (layer, 0, 0)),
                  pl.BlockSpec((None, 1, W_A), lambda i: (layer, 0, 0))],
        out_specs=pl.BlockSpec((tm, W_A), lambda i: (i, 0)),
        out_shape=jax.ShapeDtypeStruct((m, W_A), jnp.bfloat16),
        compiler_params=pltpu.CompilerParams(dimension_semantics=("parallel",),
                                             vmem_limit_bytes=VMEM_LIMIT_BYTES),
    )(y2d, w_glu, b_glu.reshape(DEPTH, 1, W_A))


ATT_T = 256
MASKED = -1e30
_CQ_BLK = (W_A + 2 * H_B * DK_B + 2 * H_B * DV_B) // (2 * DH_C)
_CK_BLK = _CQ_BLK + H_C
_CV_BLK = _CK_BLK + H_C


def _attn_bias_tiles(rel_bias):
    t = ATT_T
    assert t >= MAX_DISTANCE
    qpos = jnp.arange(3 * t, dtype=jnp.int32)
    kpos = jnp.arange(t, dtype=jnp.int32)
    bias = _bucket_lookup(rel_bias, _t5_bucket(qpos, kpos))
    bias = jnp.where((kpos[None, :] <= qpos[:, None])[None], bias, MASKED)
    return bias.reshape(H_C, 3, t, t)


def _bucket_lookup(rel_bias, bucket):
    tab = rel_bias.astype(jnp.float32)
    out = jnp.zeros((tab.shape[1],) + bucket.shape, jnp.float32)
    for i in range(N_BUCKETS):
        out = jnp.where((bucket == i)[None], tab[i].reshape((-1,) + (1,) * bucket.ndim), out)
    return out


def _softmax_update(c, s, v_bf16, m_ref, l_ref, acc_ref):
    m_prev = m_ref[c]
    m_new = jnp.maximum(m_prev, jnp.max(s, axis=-1, keepdims=True))
    alpha = jnp.exp(m_prev - m_new)
    p = jnp.exp(s - m_new)
    l_ref[c] = alpha * l_ref[c] + jnp.sum(p, axis=-1, keepdims=True)
    acc_ref[c] = alpha * acc_ref[c] + jnp.dot(p.astype(jnp.bfloat16), v_bf16,
                                              preferred_element_type=jnp.float32)
    m_ref[c] = m_new


def _head_norm(o, g, out_scale):
    return o * lax.rsqrt(jnp.mean(o * o, axis=-1, keepdims=True) + EPS) * g * out_scale


def _attn_body(lam_ref, q_ref, k_ref, v_ref, bias_ref, g_ref, o_ref, m_ref, l_ref, acc_ref, *,
               out_scale):
    qi = pl.program_id(2)
    ki = pl.program_id(3)

    @pl.when(ki == 0)
    def _():
        m_ref[...] = jnp.full_like(m_ref, MASKED)
        l_ref[...] = jnp.zeros_like(l_ref)
        acc_ref[...] = jnp.zeros_like(acc_ref)

    @pl.when(ki <= qi)
    def _():
        q = q_ref[...].astype(jnp.bfloat16)
        k = k_ref[...].astype(jnp.bfloat16)
        v = v_ref[...].astype(jnp.bfloat16)
        bias = bias_ref[...]
        for c in range(2):
            s = lax.dot_general(q[:, c * DH_C:(c + 1) * DH_C], k[:, c * DH_C:(c + 1) * DH_C], _NT,
                                preferred_element_type=jnp.float32) * (DH_C ** -0.5) + bias
            _softmax_update(c, s, v, m_ref, l_ref, acc_ref)

    @pl.when(ki == qi)
    def _():
        o = acc_ref[0] / l_ref[0] - lam_ref[...] * (acc_ref[1] / l_ref[1])
        o_ref[...] = _head_norm(o, g_ref[...], out_scale).astype(o_ref.dtype)


def _diff_attn_prompt_pallas(z2d, batch, seq, lam, bias_tiles, g_norm, out_scale):
    t = ATT_T
    nq = seq // t
    w = 2 * DH_C
    grid = (batch, H_C, nq, nq)
    return pl.pallas_call(
        functools.partial(_attn_body, out_scale=out_scale),
        grid=grid,
        in_specs=[
            pl.BlockSpec((1, 1), lambda b, h, qi, ki: (0, 0)),
            pl.BlockSpec((t, w), lambda b, h, qi, ki: (b * nq + qi, _CQ_BLK + h)),
            pl.BlockSpec((t, w), lambda b, h, qi, ki: (b * nq + jnp.minimum(ki, qi), _CK_BLK + h)),
            pl.BlockSpec((t, w), lambda b, h, qi, ki: (b * nq + jnp.minimum(ki, qi), _CV_BLK + h)),
            pl.BlockSpec((None, None, t, t),
                         lambda b, h, qi, ki: (h, jnp.clip(qi - ki, 0, 2), 0, 0)),
            pl.BlockSpec((1, w), lambda b, h, qi, ki: (0, 0)),
        ],
        out_specs=pl.BlockSpec((t, w), lambda b, h, qi, ki: (b * nq + qi, h)),
        out_shape=jax.ShapeDtypeStruct((batch * seq, H_C * w), jnp.bfloat16),
        scratch_shapes=[pltpu.VMEM((2, t, 1), jnp.float32), pltpu.VMEM((2, t, 1), jnp.float32),
                        pltpu.VMEM((2, t, w), jnp.float32)],
        compiler_params=pltpu.CompilerParams(
            dimension_semantics=("parallel", "parallel", "parallel", "arbitrary")),
    )(lam.reshape(1, 1), z2d, z2d, z2d, bias_tiles, g_norm.reshape(1, w))


def _attn_step_body(pt_ref, lam_ref, q_ref, kn_ref, vn_ref, kc_ref, vc_ref, bias_ref, bias0_ref, g_ref,
                    o_ref, m_ref, l_ref, acc_ref, *, n_pages, out_scale):
    del pt_ref
    p = pl.program_id(1)
    scale = DH_C ** -0.5
    q = q_ref[...]

    def halves(x):
        return x[..., :DH_C], x[..., DH_C:]

    @pl.when(p == 0)
    def _():
        for c, qk in enumerate(halves(q * kn_ref[...])):
            m_ref[c] = jnp.sum(qk, axis=-1, keepdims=True) * scale + bias0_ref[...]
            l_ref[c] = jnp.ones((H_C, 1), jnp.float32)
            acc_ref[c] = vn_ref[...]

    v3 = vc_ref[...]
    bias = bias_ref[...][..., 0:1]
    for c, qk in enumerate(halves(kc_ref[...] * q[None])):
        s = jnp.sum(qk, axis=-1, keepdims=True) * scale + bias
        m_prev = m_ref[c]
        m_new = jnp.maximum(m_prev, jnp.max(s, axis=0))
        alpha = jnp.exp(m_prev - m_new)
        pr = jnp.exp(s - m_new[None])
        l_ref[c] = alpha * l_ref[c] + jnp.sum(pr, axis=0)
        acc_ref[c] = alpha * acc_ref[c] + jnp.sum(pr * v3, axis=0)
        m_ref[c] = m_new

    @pl.when(p == n_pages - 1)
    def _():
        o = acc_ref[0] / l_ref[0] - lam_ref[...] * (acc_ref[1] / l_ref[1])
        o_ref[...] = _head_norm(o, g_ref[...], out_scale)


def _diff_attn_step_pallas(q, k_new, v_new, cache_k, cache_v, page_table, layer, lam, rel_bias,
                           g_norm, out_scale):
    f32 = jnp.float32
    batch, n_pages = page_table.shape
    w = 2 * DH_C
    wide = H_C * w
    past_len = n_pages * PAGE_SIZE
    assert PAGE_SIZE >= MAX_DISTANCE
    near = _t5_bucket(jnp.full((1,), past_len, jnp.int32),
                      jnp.arange(past_len - PAGE_SIZE, past_len, dtype=jnp.int32))[0]
    far = jnp.full((PAGE_SIZE,), N_BUCKETS - 1, jnp.int32)
    bias_kh = _bucket_lookup(rel_bias, jnp.stack([far, near])).transpose(1, 2, 0)
    bias_tiles = jnp.broadcast_to(bias_kh[..., None], (2, PAGE_SIZE, H_C, 128))
    bias0 = rel_bias.astype(f32)[0].reshape(H_C, 1)
    vec = lambda a: a.reshape(batch, H_C, w).astype(f32)
    row_spec = pl.BlockSpec((None, H_C, w), lambda b, p, pt: (b, 0, 0))
    page_spec = pl.BlockSpec((None, None, PAGE_SIZE, H_C, w), lambda b, p, pt: (layer, pt[b, p], 0, 0, 0))
    out = pl.pallas_call(
        functools.partial(_attn_step_body, n_pages=n_pages, out_scale=out_scale),
        grid_spec=pltpu.PrefetchScalarGridSpec(
            num_scalar_prefetch=1,
            grid=(batch, n_pages),
            in_specs=[
                pl.BlockSpec((1, 1), lambda b, p, pt: (0, 0)),
                row_spec, row_spec, row_spec,
                page_spec, page_spec,
                pl.BlockSpec((None, PAGE_SIZE, H_C, 128), lambda b, p, pt: (p // (n_pages - 1), 0, 0, 0)),
                pl.BlockSpec((H_C, 1), lambda b, p, pt: (0, 0)),
                pl.BlockSpec((1, w), lambda b, p, pt: (0, 0)),
            ],
            out_specs=pl.BlockSpec((None, H_C, w), lambda b, p, pt: (b, 0, 0)),
            scratch_shapes=[pltpu.VMEM((2, H_C, 1), f32), pltpu.VMEM((2, H_C, 1), f32),
                            pltpu.VMEM((2, H_C, w), f32)],
        ),
        out_shape=jax.ShapeDtypeStruct((batch, H_C, w), f32),
        compiler_params=pltpu.CompilerParams(dimension_semantics=("parallel", "arbitrary")),
    )(page_table, lam.reshape(1, 1), vec(q), vec(k_new), vec(v_new), cache_k, cache_v, bias_tiles, bias0,
      g_norm.reshape(1, w))
    return out.reshape(batch, 1, wide)


_DQ_BLK = (N_IN_MAIN - H_D * DV_D - W_D_QKV) // GROUP_WIDTH
_DG_BLK = (N_IN_MAIN - H_D * DV_D) // GROUP_WIDTH
CONV_CARRY = 8


def _silu(x):
    return x * jax.nn.sigmoid(x)


def _softplus(x):
    return jnp.maximum(x, 0.0) + jnp.log(1.0 + jnp.exp(-jnp.abs(x)))


def _cumsum_rows(x):
    n = x.shape[0]
    row = lax.broadcasted_iota(jnp.int32, x.shape, 0)
    sh = 1
    while sh < n:
        x = x + jnp.where(row >= sh, pltpu.roll(x, sh, axis=0), 0.0)
        sh *= 2
    return x


def _unit_lower_inverse(ms):
    n = ms[0].shape[0]
    r = lax.broadcasted_iota(jnp.int32, (n, n), 0)
    c = lax.broadcasted_iota(jnp.int32, (n, n), 1)
    eye = (r == c).astype(jnp.float32)
    ts = [eye - jnp.where(r // 2 == c // 2, m, 0.0) for m in ms]
    sz = 2
    while sz < n:
        off = (r // (2 * sz) == c // (2 * sz)) & (r // sz != c // sz)
        tm = [_dot3(t, jnp.where(off, m, 0.0)) for t, m in zip(ts, ms)]
        ts = [t - _dot3(x, t) for t, x in zip(ts, tm)]
        sz *= 2
    return ts


def _l2n(x):
    return x * lax.rsqrt(jnp.sum(x * x, axis=-1, keepdims=True) + EPS)


def _gdn_body(xq_ref, xk_ref, xv_ref, gate_ref, tail_ref, cw_ref, ab_ref, ng_ref,
              o_ref, sout_ref, s_ref, carry_ref, *, n_chunks):
    bf16 = jnp.bfloat16
    L = CHUNK
    ci = pl.program_id(1)

    @pl.when(ci == 0)
    def _():
        s_ref[...] = jnp.zeros_like(s_ref)
        carry_ref[...] = jnp.zeros_like(carry_ref)

    def conv(x_ref, part):
        x = x_ref[...]
        cols = slice(part * GROUP_WIDTH, (part + 1) * GROUP_WIDTH)
        xc = jnp.concatenate([carry_ref[:, cols], x], axis=0)
        w = cw_ref[:, cols]
        base = CONV_CARRY - (CONV_W - 1)
        y = xc[base:base + L] * w[0:1]
        for j in range(1, CONV_W):
            y = y + xc[base + j:base + j + L] * w[j:j + 1]
        carry_ref[:, cols] = x[L - CONV_CARRY:L]
        return _silu(y)

    qc, kc, vc = conv(xq_ref, 0), conv(xk_ref, 1), conv(xv_ref, 2)
    gate = gate_ref[...]
    tail = tail_ref[...]
    beta_all = jax.nn.sigmoid(tail)
    logg = -jnp.exp(ab_ref[0:1, :]) * _softplus(tail + ab_ref[1:2, :])
    gcum_all = _cumsum_rows(logg)
    gcum_t = jnp.concatenate([gcum_all, jnp.zeros((128 - L, 128), jnp.float32)], axis=0).T[:, :L]

    r = lax.broadcasted_iota(jnp.int32, (L, L), 0)
    c = lax.broadcasted_iota(jnp.int32, (L, L), 1)
    tri = r >= c
    strict = r > c

    heads = range(H_D)
    hsl = [slice(h * DK_D, (h + 1) * DK_D) for h in heads]
    states = [s_ref[h] for h in heads]
    q = [_l2n(qc[:, hsl[h]]) * (DK_D ** -0.5) for h in heads]
    k = [_l2n(kc[:, hsl[h]]) for h in heads]
    beta = [beta_all[:, h:h + 1] for h in heads]
    g_col = [gcum_all[:, H_D + h:H_D + h + 1] for h in heads]
    ldec = [jnp.exp(jnp.minimum(g_col[h] - gcum_t[H_D + h:H_D + h + 1, :], 0.0)) for h in heads]
    kb = [k[h] * beta[h] for h in heads]
    k16 = [k[h].astype(bf16) for h in heads]
    m = [jnp.where(strict, lax.dot_general(kb[h].astype(bf16), k16[h], _NT,
                                           preferred_element_type=jnp.float32) * ldec[h], 0.0)
         for h in heads]
    eg = [jnp.exp(g_col[h]) for h in heads]
    rhs = [jnp.concatenate([vc[:, hsl[h]] * beta[h], kb[h] * eg[h]], axis=1) for h in heads]
    tinv = _unit_lower_inverse(m)
    sol = [_dot3(tinv[h], rhs[h]) for h in heads]
    s16 = [states[h].astype(bf16) for h in heads]
    v_new = [sol[h][:, :DV_D] - jnp.dot(sol[h][:, DV_D:].astype(bf16), s16[h],
                                        preferred_element_type=jnp.float32) for h in heads]
    att = [jnp.where(tri, lax.dot_general(q[h].astype(bf16), k16[h], _NT,
                                          preferred_element_type=jnp.float32) * ldec[h], 0.0)
           for h in heads]
    vn16 = [v_new[h].astype(bf16) for h in heads]
    o = [jnp.dot((q[h] * eg[h]).astype(bf16), s16[h], preferred_element_type=jnp.float32)
         + jnp.dot(att[h].astype(bf16), vn16[h], preferred_element_type=jnp.float32) for h in heads]
    for h in heads:
        g_last = g_col[h][L - 1:L, :]
        kdec = k[h] * jnp.exp(g_last - g_col[h])
        s_ref[h] = jnp.exp(g_last) * states[h] + lax.dot_general(
            kdec.astype(bf16), vn16[h], _TN, preferred_element_type=jnp.float32)
        on = o[h] * lax.rsqrt(jnp.mean(o[h] * o[h], axis=-1, keepdims=True) + EPS) * ng_ref[...]
        o_ref[:, hsl[h]] = (on * _silu(gate[:, hsl[h]])).astype(o_ref.dtype)

    @pl.when(ci == n_chunks - 1)
    def _():
        sout_ref[...] = s_ref[...]


def _gdn_prompt(z2d, zt2d, batch, seq, conv_w, a_log, dt_bias, norm_g):
    L = CHUNK
    nc = seq // L
    gw = GROUP_WIDTH
    ab = jnp.zeros((2, 128), jnp.float32)
    ab = ab.at[0, H_D:2 * H_D].set(a_log.astype(jnp.float32)).at[1, H_D:2 * H_D].set(dt_bias.astype(jnp.float32))
    zspec = lambda blk: pl.BlockSpec((L, gw), lambda b, ci: (b * nc + ci, blk))
    const = lambda shape: pl.BlockSpec(shape, lambda b, ci: (0,) * len(shape))
    return pl.pallas_call(
        functools.partial(_gdn_body, n_chunks=nc),
        grid=(batch, nc),
        in_specs=[zspec(_DQ_BLK), zspec(_DQ_BLK + 1), zspec(_DQ_BLK + 2), zspec(_DG_BLK),
                  pl.BlockSpec((L, 128), lambda b, ci: (b * nc + ci, 0)),
                  const((CONV_W, W_D_QKV)), const((2, 128)), const((1, DV_D))],
        out_specs=[pl.BlockSpec((L, gw), lambda b, ci: (b * nc + ci, 0)),
                   pl.BlockSpec((None, H_D, DK_D, DV_D), lambda b, ci: (b, 0, 0, 0))],
        out_shape=[jax.ShapeDtypeStruct((batch * seq, gw), jnp.bfloat16),
                   jax.ShapeDtypeStruct((batch, H_D, DK_D, DV_D), jnp.float32)],
        scratch_shapes=[pltpu.VMEM((H_D, DK_D, DV_D), jnp.float32),
                        pltpu.VMEM((CONV_CARRY, W_D_QKV), jnp.float32)],
        compiler_params=pltpu.CompilerParams(dimension_semantics=("parallel", "arbitrary")),
    )(z2d, z2d, z2d, z2d, zt2d, conv_w.astype(jnp.float32), ab, norm_g.reshape(1, DV_D).astype(jnp.float32))


_BQ_BLK = W_A // GROUP_WIDTH
_STEP_VECS = 8


def _logaddexp(a, b):
    return jnp.maximum(a, b) + jnp.log(1.0 + jnp.exp(-jnp.abs(a - b)))


def _log_sigmoid(x):
    return jnp.minimum(x, 0.0) - jnp.log(1.0 + jnp.exp(-jnp.abs(x)))


def _gla_body(bq_ref, bf_ref, bi_ref, bg_ref, lb_ref, ng_ref, o_ref, sout_ref, st_ref, *, n_chunks):
    bf16 = jnp.bfloat16
    f32 = jnp.float32
    L = CHUNK
    ci = pl.program_id(1)

    @pl.when(ci == 0)
    def _():
        st_ref[...] = jnp.zeros_like(st_ref)

    levels = []
    m = L // 2
    while m >= 1:
        levels.append(m)
        m //= 2
    nl = len(levels)
    rr = lax.broadcasted_iota(jnp.int32, (nl * L, L), 0)
    cc = lax.broadcasted_iota(jnp.int32, (nl * L, L), 1)
    tt = rr % L
    sel = jnp.zeros((nl * L, L), f32)
    for i, m in enumerate(levels):
        sel = jnp.where((rr // L == i) & (cc == (tt // (2 * m)) * (2 * m) + m - 1), 1.0, sel)
    sel = sel.astype(bf16)
    r = lax.broadcasted_iota(jnp.int32, (L, L), 0)
    c = lax.broadcasted_iota(jnp.int32, (L, L), 1)
    row = lax.broadcasted_iota(jnp.int32, (L, 1), 0)

    heads = range(H_B)
    hsl = [slice(h * DK_B, (h + 1) * DK_B) for h in heads]
    bq, fl_all, bi, bg = bq_ref[...], bf_ref[...], bi_ref[...], bg_ref[...]
    logf_all = _logaddexp(lb_ref[0:1, :], lb_ref[1:2, :] + _log_sigmoid(fl_all))
    b_all = _cumsum_rows(logf_all)
    q_all = _silu(bq) * (DK_B ** -0.5)
    k_all = lb_ref[2:3, :] * jax.nn.sigmoid(-fl_all)

    states = [st_ref[h] for h in heads]
    q = [q_all[:, hsl[h]] for h in heads]
    k = [k_all[:, hsl[h]] for h in heads]
    b = [b_all[:, hsl[h]] for h in heads]
    v16 = [bi[:, hsl[h]].astype(bf16) for h in heads]

    def pieces(x):
        p1 = x.astype(bf16)
        r1 = x - p1.astype(f32)
        p2 = r1.astype(bf16)
        p3 = (r1 - p2.astype(f32)).astype(bf16)
        return p1, p2, p3

    def pick(h):
        p1, p2, p3 = pieces(b[h])
        d = lambda p: jnp.dot(sel, p, preferred_element_type=f32)
        return (d(p1) + d(p2)) + d(p3)

    refs = [pick(h) for h in heads]
    att = [jnp.where(r == c, jnp.sum(q[h] * k[h], axis=-1, keepdims=True), 0.0) for h in heads]
    for i, m in enumerate(levels):
        upper = (row // m) % 2 == 1
        quad = (r // (2 * m) == c // (2 * m)) & ((r // m) % 2 == 1) & ((c // m) % 2 == 0)
        for h in heads:
            ref = refs[h][i * L:(i + 1) * L, :]
            qf = jnp.where(upper, q[h] * jnp.exp(jnp.minimum(b[h] - ref, 0.0)), 0.0)
            kf = jnp.where(upper, 0.0, k[h] * jnp.exp(jnp.minimum(ref - b[h], 0.0)))
            pm = lax.dot_general(qf.astype(bf16), kf.astype(bf16), _NT, preferred_element_type=f32)
            att[h] = att[h] + jnp.where(quad, pm, 0.0)
    s16 = [states[h].astype(bf16) for h in heads]
    o = [lax.dot_general((q[h] * jnp.exp(b[h])).astype(bf16), s16[h], _NT, preferred_element_type=f32)
         + jnp.dot(att[h].astype(bf16), v16[h], preferred_element_type=f32) for h in heads]
    for h in heads:
        b_last = b[h][L - 1:L, :]
        kdec = k[h] * jnp.exp(b_last - b[h])
        st_ref[h] = jnp.exp(b_last) * states[h] + lax.dot_general(v16[h], kdec.astype(bf16), _TN,
                                                                  preferred_element_type=f32)
        on = o[h] * lax.rsqrt(jnp.mean(o[h] * o[h], axis=-1, keepdims=True) + EPS) * ng_ref[...]
        o_ref[:, hsl[h]] = (on * _silu(bg[:, hsl[h]])).astype(o_ref.dtype)

    @pl.when(ci == n_chunks - 1)
    def _():
        for h in heads:
            sout_ref[h] = st_ref[h].T


def _gla_prompt(z2d, batch, seq, lb, norm_g):
    f32 = jnp.float32
    L = CHUNK
    nc = seq // L
    gw = GROUP_WIDTH
    lbf = lb.astype(f32).reshape(1, gw)
    lb_rows = jnp.concatenate([jnp.log(lbf), jnp.log1p(-lbf), 1.0 - lbf], axis=0)
    zspec = lambda blk: pl.BlockSpec((L, gw), lambda b, ci: (b * nc + ci, blk))
    const = lambda shape: pl.BlockSpec(shape, lambda b, ci: (0,) * len(shape))
    return pl.pallas_call(
        functools.partial(_gla_body, n_chunks=nc),
        grid=(batch, nc),
        in_specs=[zspec(_BQ_BLK), zspec(_BQ_BLK + 1), zspec(_BQ_BLK + 2), zspec(_BQ_BLK + 3),
                  const((3, gw)), const((1, DV_B))],
        out_specs=[pl.BlockSpec((L, gw), lambda b, ci: (b * nc + ci, 0)),
                   pl.BlockSpec((None, H_B, DK_B, DV_B), lambda b, ci: (b, 0, 0, 0))],
        out_shape=[jax.ShapeDtypeStruct((batch * seq, gw), jnp.bfloat16),
                   jax.ShapeDtypeStruct((batch, H_B, DK_B, DV_B), f32)],
        scratch_shapes=[pltpu.VMEM((H_B, DV_B, DK_B), f32)],
        compiler_params=pltpu.CompilerParams(dimension_semantics=("parallel", "arbitrary")),
    )(z2d, z2d, z2d, z2d, lb_rows, norm_g.reshape(1, DV_B).astype(f32))


def _rec_step_body(bq_ref, bf_ref, bi_ref, bg_ref, xq_ref, xk_ref, xv_ref, dg_ref, tail_ref, buf_ref,
                   sb_ref, sd_ref, lb_ref, cw_ref, ab_ref, nb_ref, nd_ref,
                   ob_ref, od_ref, sbo_ref, sdo_ref):
    gw = GROUP_WIDTH
    q_b = _silu(bq_ref[...]) * (DK_B ** -0.5)
    fl = bf_ref[...]
    dec_b = jnp.exp(_logaddexp(lb_ref[0:1, :], lb_ref[1:2, :] + _log_sigmoid(fl)))
    k_b = lb_ref[2:3, :] * jax.nn.sigmoid(-fl)
    v_b = bi_ref[...]
    buf = buf_ref[...]

    def conv(x_ref, part):
        cols = slice(part * gw, (part + 1) * gw)
        y = x_ref[...] * cw_ref[CONV_W - 1:CONV_W, cols]
        for j in range(CONV_W - 1):
            y = y + buf[j:j + 1, cols] * cw_ref[j:j + 1, cols]
        return _silu(y)

    qd, kd, vd = conv(xq_ref, 0), conv(xk_ref, 1), conv(xv_ref, 2)
    tail = tail_ref[...]
    beta_all = jax.nn.sigmoid(tail)
    eg_all = jnp.exp(-jnp.exp(ab_ref[0:1, :]) * _softplus(tail + ab_ref[1:2, :]))

    rows = []
    for h in range(H_B):
        hs = slice(h * DK_B, (h + 1) * DK_B)
        rows += [dec_b[:, hs], k_b[:, hs], q_b[:, hs],
                 _l2n(kd[:, hs]), _l2n(qd[:, hs]) * (DK_D ** -0.5),
                 jnp.zeros((_STEP_VECS - 5, DK_B), jnp.float32)]
    rows.append(jnp.zeros((128 - H_B * _STEP_VECS, DK_B), jnp.float32))
    cols_t = jnp.concatenate(rows, axis=0).T

    for h in range(H_B):
        hs = slice(h * DK_B, (h + 1) * DK_B)
        c0 = h * _STEP_VECS
        dec_c, k_c, q_c = cols_t[:, c0:c0 + 1], cols_t[:, c0 + 1:c0 + 2], cols_t[:, c0 + 2:c0 + 3]
        kd_c, qd_c = cols_t[:, c0 + 3:c0 + 4], cols_t[:, c0 + 4:c0 + 5]
        s_new = dec_c * sb_ref[h] + k_c * v_b[:, hs]
        sbo_ref[h] = s_new
        o = jnp.sum(q_c * s_new, axis=0, keepdims=True)
        ob_ref[:, hs] = _head_norm(o, nb_ref[...], 1.0) * _silu(bg_ref[:, hs])
        s = sd_ref[h]
        eg = eg_all[:, H_D + h:H_D + h + 1]
        beta = beta_all[:, h:h + 1]
        ks = jnp.sum(kd_c * s, axis=0, keepdims=True)
        v_new = beta * (vd[:, hs] - eg * ks)
        s_new = eg * s + kd_c * v_new
        sdo_ref[h] = s_new
        o = jnp.sum(qd_c * s_new, axis=0, keepdims=True)
        od_ref[:, hs] = _head_norm(o, nd_ref[...], 1.0) * _silu(dg_ref[:, hs])


def _rec_step(z2d, zt2d, conv_buf, s_b, s_d, lb, conv_w, a_log, dt_bias, norm_b, norm_d):
    f32 = jnp.float32
    batch = z2d.shape[0]
    gw = GROUP_WIDTH
    z3 = z2d.reshape(batch, 1, N_IN_MAIN)
    zt3 = zt2d.reshape(batch, 1, 128)
    lbf = lb.astype(f32).reshape(1, gw)
    lb_rows = jnp.concatenate([jnp.log(lbf), jnp.log1p(-lbf), 1.0 - lbf], axis=0)
    ab = jnp.zeros((2, 128), f32)
    ab = ab.at[0, H_D:2 * H_D].set(a_log.astype(f32)).at[1, H_D:2 * H_D].set(dt_bias.astype(f32))
    zspec = lambda blk: pl.BlockSpec((None, 1, gw), lambda b: (b, 0, blk))
    const = lambda shape: pl.BlockSpec(shape, lambda b: (0,) * len(shape))
    st_spec = pl.BlockSpec((None, H_B, DK_B, DV_B), lambda b: (b, 0, 0, 0))
    o_spec = pl.BlockSpec((None, 1, gw), lambda b: (b, 0, 0))
    ob, od, sbo, sdo = pl.pallas_call(
        _rec_step_body,
        grid=(batch,),
        in_specs=[zspec(_BQ_BLK), zspec(_BQ_BLK + 1), zspec(_BQ_BLK + 2), zspec(_BQ_BLK + 3),
                  zspec(_DQ_BLK), zspec(_DQ_BLK + 1), zspec(_DQ_BLK + 2), zspec(_DG_BLK),
                  pl.BlockSpec((None, 1, 128), lambda b: (b, 0, 0)),
                  pl.BlockSpec((None, CONV_W - 1, W_D_QKV), lambda b: (b, 0, 0)),
                  st_spec, st_spec,
                  const((3, gw)), const((CONV_W, W_D_QKV)), const((2, 128)),
                  const((1, DV_B)), const((1, DV_D))],
        out_specs=[o_spec, o_spec, st_spec, st_spec],
        out_shape=[jax.ShapeDtypeStruct((batch, 1, gw), f32), jax.ShapeDtypeStruct((batch, 1, gw), f32),
                   jax.ShapeDtypeStruct(s_b.shape, f32), jax.ShapeDtypeStruct(s_d.shape, f32)],
        compiler_params=pltpu.CompilerParams(dimension_semantics=("parallel",)),
    )(z3, z3, z3, z3, z3, z3, z3, z3, zt3, conv_buf.astype(f32), s_b.astype(f32), s_d.astype(f32),
      lb_rows, conv_w.astype(f32), ab, norm_b.reshape(1, DV_B).astype(f32), norm_d.reshape(1, DV_D).astype(f32))
    return ob, od, sbo, sdo


def _rms_norm(x, g):
    xf = x.astype(jnp.float32)
    y = xf * lax.rsqrt(jnp.mean(xf * xf, axis=-1, keepdims=True) + EPS)
    return (y * g.astype(jnp.float32)).astype(x.dtype)


def _l2norm(x):
    return x * lax.rsqrt(jnp.sum(x * x, axis=-1, keepdims=True) + EPS)


def _complex_affine_combine(e1, e2):
    a1r, a1i, b1r, b1i = e1
    a2r, a2i, b2r, b2i = e2
    return (a2r * a1r - a2i * a1i, a2r * a1i + a2i * a1r,
            a2r * b1r - a2i * b1i + b2r, a2r * b1i + a2i * b1r + b2i)


def _s5(u, lam_re, lam_im, log_dt, b_re, b_im, c_re, c_im, d_skip, h0_re, h0_im):
    f32 = jnp.float32
    B, T, _ = u.shape
    uf = u.astype(f32).reshape(B, T, G_A, SSM_CH)
    lr, li = lam_re.astype(f32), lam_im.astype(f32)
    dt = jnp.exp(log_dt.astype(f32))[:, None]
    mag = jnp.exp(lr * dt)
    ab_re, ab_im = mag * jnp.cos(li * dt), mag * jnp.sin(li * dt)
    den = lr * lr + li * li
    fr = ((ab_re - 1.0) * lr + ab_im * li) / den
    fi = (ab_im * lr - (ab_re - 1.0) * li) / den
    br, bi = b_re.astype(f32), b_im.astype(f32)
    bb_re = fr[..., None] * br - fi[..., None] * bi
    bb_im = fr[..., None] * bi + fi[..., None] * br
    bu_re = jnp.einsum('btgc,gpc->btgp', uf, bb_re)
    bu_im = jnp.einsum('btgc,gpc->btgp', uf, bb_im)
    a_re = jnp.broadcast_to(ab_re, bu_re.shape)
    a_im = jnp.broadcast_to(ab_im, bu_im.shape)
    acc_re, acc_im, s_re, s_im = lax.associative_scan(
        _complex_affine_combine, (a_re, a_im, bu_re, bu_im), axis=1)
    h0r = h0_re.astype(f32)[:, None]
    h0i = h0_im.astype(f32)[:, None]
    hr = s_re + acc_re * h0r - acc_im * h0i
    hi = s_im + acc_re * h0i + acc_im * h0r
    y = (jnp.einsum('gcp,btgp->btgc', c_re.astype(f32), hr)
         - jnp.einsum('gcp,btgp->btgc', c_im.astype(f32), hi)
         + d_skip.astype(f32) * uf).reshape(B, T, W_A)
    return jax.nn.gelu(y), hr[:, -1], hi[:, -1]


def _to_chunks(a, L, n):
    B, T = a.shape[:2]
    a = jnp.pad(a, [(0, 0), (0, n * L - T)] + [(0, 0)] * (a.ndim - 2))
    a = a.reshape((B, n, L) + a.shape[2:])
    a = jnp.moveaxis(a, 1, 0)
    return jnp.swapaxes(a, 2, 3)


def _from_chunks(o, T):
    n, B, H, L = o.shape[:4]
    o = jnp.moveaxis(jnp.swapaxes(o, 2, 3), 0, 1)
    return o.reshape((B, n * L) + o.shape[3:])[:, :T]


def _gla_chunked(q, k, v, logf, S0):
    T = q.shape[1]
    L = min(CHUNK, T)
    n = -(-T // L)
    qc, kc, vc, gc = (_to_chunks(a, L, n) for a in (q, k, v, logf))
    tri = jnp.tril(jnp.ones((L, L), dtype=bool))

    def step(S, inp):
        qi, ki, vi, gi = inp
        b = jnp.cumsum(gi, axis=2)
        dec = jnp.exp(jnp.where(tri[:, :, None], b[:, :, :, None, :] - b[:, :, None, :, :], -jnp.inf))
        att = jnp.sum(qi[:, :, :, None, :] * ki[:, :, None, :, :] * dec, axis=-1)
        o = (jnp.einsum('bhtk,bhkv->bhtv', qi * jnp.exp(b), S)
             + jnp.einsum('bhts,bhsv->bhtv', att, vi))
        b_last = b[:, :, -1:, :]
        S_new = (jnp.exp(b_last[:, :, 0, :])[..., None] * S
                 + jnp.einsum('bhlk,bhlv->bhkv', ki * jnp.exp(b_last - b), vi))
        return S_new, o

    S, o = lax.scan(step, S0, (qc, kc, vc, gc))
    return _from_chunks(o, T), S


def _gdn_chunked(q, k, v, beta, logg, S0):
    T = q.shape[1]
    V = v.shape[-1]
    L = min(CHUNK, T)
    n = -(-T // L)
    qc, kc, vc, bc, gc = (_to_chunks(a, L, n) for a in (q, k, v, beta, logg))
    tri = jnp.tril(jnp.ones((L, L), dtype=bool))
    strict = jnp.tril(jnp.ones((L, L), dtype=bool), -1)
    eye = jnp.eye(L, dtype=jnp.float32)

    def step(S, inp):
        qi, ki, vi, bi, gi = inp
        gcum = jnp.cumsum(gi, axis=-1)
        ldec = jnp.exp(jnp.where(tri, gcum[..., :, None] - gcum[..., None, :], -jnp.inf))
        kb = ki * bi[..., None]
        m = jnp.where(strict, jnp.einsum('bhtk,bhsk->bhts', kb, ki) * ldec, 0.0)
        rhs = jnp.concatenate([vi * bi[..., None], kb * jnp.exp(gcum)[..., None]], axis=-1)
        sol = lax.linalg.triangular_solve(eye + m, rhs, left_side=True, lower=True, unit_diagonal=True)
        u, w = sol[..., :V], sol[..., V:]
        v_new = u - jnp.einsum('bhlk,bhkv->bhlv', w, S)
        att = jnp.einsum('bhtk,bhsk->bhts', qi, ki) * ldec
        o = (jnp.einsum('bhlk,bhkv->bhlv', qi * jnp.exp(gcum)[..., None], S)
             + jnp.einsum('bhts,bhsv->bhtv', att, v_new))
        g_last = gcum[..., -1]
        S_new = (jnp.exp(g_last)[..., None, None] * S
                 + jnp.einsum('bhlk,bhlv->bhkv', ki * jnp.exp(g_last[..., None] - gcum)[..., None], v_new))
        return S_new, o

    S, o = lax.scan(step, S0, (qc, kc, vc, bc, gc))
    return _from_chunks(o, T), S


def _causal_conv(x, buf, w):
    T = x.shape[1]
    xx = jnp.concatenate([buf.astype(x.dtype), x], axis=1)
    y = xx[:, 0:T] * w[0]
    for j in range(1, CONV_W):
        y = y + xx[:, j:j + T] * w[j]
    return jax.nn.silu(y), xx[:, T:]


def _t5_bucket(qpos, kpos):
    n = jnp.maximum(qpos[:, None] - kpos[None, :], 0)
    max_exact = N_BUCKETS // 2
    nf = jnp.maximum(n, 1).astype(jnp.float32)
    large = max_exact + (jnp.log(nf / max_exact) / math.log(MAX_DISTANCE / max_exact)
                         * (N_BUCKETS - max_exact)).astype(jnp.int32)
    large = jnp.minimum(large, N_BUCKETS - 1)
    return jnp.where(n < max_exact, n, large)


def _diff_attn_core(q, k, v, qpos, kpos, lam, rel_table):
    s = jnp.einsum('bqhcd,bkhcd->bchqk', q, k) * (DH_C ** -0.5)
    bias = jnp.transpose(rel_table.astype(jnp.float32)[_t5_bucket(qpos, kpos)], (2, 0, 1))
    causal = kpos[None, :] <= qpos[:, None]
    s = jnp.where(causal, s + bias, -jnp.inf)
    p = jax.nn.softmax(s, axis=-1)
    a = p[:, 0] - lam * p[:, 1]
    return jnp.einsum('bhqk,bkhe->bqhe', a, v)


def _diff_attn_prompt(q, k, v, lam, rel_table):
    B, T = q.shape[:2]
    qb = min(Q_BLOCK, T)
    nb = T // qb
    q_blocks = jnp.moveaxis(q.reshape((B, nb, qb) + q.shape[2:]), 1, 0)
    qpos = jnp.arange(T, dtype=jnp.int32).reshape(nb, qb)
    kpos = jnp.arange(T, dtype=jnp.int32)
    out = lax.map(lambda blk: _diff_attn_core(blk[0], k, v, blk[1], kpos, lam, rel_table), (q_blocks, qpos))
    return jnp.moveaxis(out, 0, 1).reshape((B, T) + out.shape[3:])


def _split_main():
    sizes = [W_A, H_B * DK_B, H_B * DK_B, H_B * DV_B, H_B * DV_B,
             2 * H_C * DH_C, 2 * H_C * DH_C, 2 * H_C * DH_C, W_D_QKV, H_D * DV_D]
    pts, acc = [], 0
    for s in sizes[:-1]:
        acc += s
        pts.append(acc)
    return pts


def _layer(l, x, mod, lb, p, past):
    f32 = jnp.float32
    bf16 = jnp.bfloat16
    B, T, _ = x.shape
    M = B * T
    sh1, sc1, gt1, sh2, sc2, gt2 = jnp.split(mod[:, None, :], 6, axis=-1)
    Mp = max(M, 16)
    tm, tn, tk = _tiles(Mp)

    def pad_rows(a):
        return a if Mp == M else jnp.pad(a, ((0, Mp - M), (0, 0)))

    h2d = pad_rows(_norm_mod(x.reshape(M, D_MODEL), p['g_norm1'][l], sc1, sh1, T, bf16))
    z = _matmul(h2d, p['w_in'], l, N_IN_MAIN, tm=tm, tn=tn, tk=tk, out_dtype=f32)
    zt = _matmul(h2d, p['w_in_tail'], l, 128, tm=tm, tn=128, tk=tk, out_dtype=f32)
    z2d = z[:M]
    z = z2d.reshape(B, T, N_IN_MAIN)
    zt = zt[:M].reshape(B, T, 128)
    (a_u, b_q, b_f, b_i, b_g, c_q, c_k, c_v, d_qkv, d_g) = jnp.split(z, _split_main(), axis=-1)
    d_beta, d_a = zt[..., :H_D], zt[..., H_D:2 * H_D]

    if past is None:
        h0_re = jnp.zeros((B, G_A, P_A), f32)
        h0_im = jnp.zeros((B, G_A, P_A), f32)
        s0_b = jnp.zeros((B, H_B, DK_B, DV_B), f32)
        s0_d = jnp.zeros((B, H_D, DK_D, DV_D), f32)
        conv0 = jnp.zeros((B, CONV_W - 1, W_D_QKV), z.dtype)
    else:
        h0_re, h0_im = past['ssm_re'][l], past['ssm_im'][l]
        s0_b, s0_d, conv0 = past['hgrn'][l], past['gdn'][l], past['conv'][l]

    s5_params = (p['ssm_lam_re'][l], p['ssm_lam_im'][l], p['ssm_log_dt'][l], p['ssm_b_re'][l],
                 p['ssm_b_im'][l], p['ssm_c_re'][l], p['ssm_c_im'][l], p['ssm_d'][l])
    if past is None:
        y_a, hr, hi = _s5_prompt(a_u, _s5_tables(*s5_params, T // S5_L))
    else:
        y_a, hr, hi = _s5_step(a_u, h0_re, h0_im, _s5_tables(*s5_params, 1, L=1))
    o_a = _s5_glu(pad_rows(y_a.reshape(M, W_A)), p['ssm_w_glu'], p['ssm_b_glu'], l,
                  min(Mp, 512))[:M].reshape(B, T, W_A)

    if past is None:
        o_b, s_b = _gla_prompt(z2d, B, T, lb, p['hgrn_norm'][l])
    else:
        o_b, o_d, s_b, s_d = _rec_step(z2d, zt[:, 0, :], conv0, s0_b, s0_d, lb, p['gdn_conv'][l],
                                       p['gdn_a_log'][l], p['gdn_dt_bias'][l], p['hgrn_norm'][l],
                                       p['gdn_norm'][l])

    lam_init = 0.8 - 0.6 * math.exp(-0.3 * l)
    lam = (jnp.exp(jnp.sum(p['diff_lq1'][l].astype(f32) * p['diff_lk1'][l].astype(f32)))
           - jnp.exp(jnp.sum(p['diff_lq2'][l].astype(f32) * p['diff_lk2'][l].astype(f32))) + lam_init)
    kh = c_k.reshape(B, T, H_C, 2 * DH_C)
    vh = c_v.reshape(B, T, H_C, 2 * DH_C)
    if past is None:
        o_c = _diff_attn_prompt_pallas(z2d, B, T, lam, _attn_bias_tiles(p['rel_bias']),
                                       p['diff_norm'][l], 1.0 - lam_init)
    else:
        o_c = _diff_attn_step_pallas(c_q.reshape(M, -1), c_k.reshape(M, -1), c_v.reshape(M, -1),
                                     past['cache_k'], past['cache_v'], past['page_table'], l, lam,
                                     p['rel_bias'], p['diff_norm'][l], 1.0 - lam_init)
    o_c = o_c.reshape(B, T, 2 * H_C * DH_C)

    conv_new = jnp.concatenate([conv0.astype(d_qkv.dtype), d_qkv], axis=1)[:, T:]
    if past is None:
        o_d, s_d = _gdn_prompt(z2d, zt.reshape(M, 128), B, T, p['gdn_conv'][l], p['gdn_a_log'][l],
                               p['gdn_dt_bias'][l], p['gdn_norm'][l])
    o_b = o_b.reshape(B, T, H_B * DV_B)
    o_d = o_d.reshape(B, T, H_D * DV_D)

    o = pad_rows(jnp.concatenate([o_a, o_b, o_c, o_d], axis=-1).reshape(M, D_MODEL).astype(bf16))
    x2d = pad_rows(x.reshape(M, D_MODEL))

    def gate_rows(g):
        if T < tm:
            return pad_rows(jnp.broadcast_to(g, (B, T, D_MODEL)).reshape(M, D_MODEL)), \
                pl.BlockSpec((tm, tn), lambda i, j, k: (i, j))
        per = T // tm
        return g.reshape(B, 1, D_MODEL), \
            pl.BlockSpec((None, 1, tn), lambda i, j, k: (i // per, 0, j))

    g1, g1_spec = gate_rows(gt1)
    x2d = _matmul(o, p['w_out'], l, D_MODEL, tm=tm, tn=tn, tk=tk, out_dtype=f32,
                  epilogue="gated_residual", extras=(x2d, g1),
                  extra_specs=(pl.BlockSpec((tm, tn), lambda i, j, k: (i, j)), g1_spec))

    h2 = pad_rows(_norm_mod(x2d[:M], p['g_norm2'][l], sc2, sh2, T, bf16))
    ff = _matmul(h2, p['w_up'], l, D_FF, tm=tm, tn=tn, tk=tk, out_dtype=bf16, epilogue="relu2")
    g2, g2_spec = gate_rows(gt2)
    x2d = _matmul(ff, p['w_down'], l, D_MODEL, tm=tm, tn=tn, tk=tk, out_dtype=f32,
                  epilogue="gated_residual", extras=(x2d, g2),
                  extra_specs=(pl.BlockSpec((tm, tn), lambda i, j, k: (i, j)), g2_spec))
    return x2d[:M].reshape(B, T, D_MODEL), (kh, vh, hr, hi, s_b, s_d, conv_new)


def _trunk(x, mods, p, past):
    sm = jax.nn.softmax(p['hgrn_lb_logits'].astype(jnp.float32), axis=0)
    lb_all = jnp.cumsum(sm, axis=0)
    lb_all = lb_all - lb_all[0:1]
    cols = [[] for _ in range(7)]
    for l in range(DEPTH):
        x, st = _layer(l, x, mods[l], lb_all[l], p, past)
        for j in range(7):
            cols[j].append(st[j])
    nb, nt, _ = x.shape
    zero = jnp.zeros((nb, D_MODEL), jnp.float32)
    y = _norm_mod(x.reshape(nb * nt, D_MODEL), p['g_final'], zero, zero, nt, x.dtype).reshape(x.shape)
    return y, [jnp.stack(col, axis=0) for col in cols]


def kernel(x_prompt, x_sample, c_prompt, c_sample, cache_k, cache_v, page_table, state_ssm_re, state_ssm_im, state_hgrn, state_gdn, state_gdn_conv, w_ada, b_ada, g_norm1, w_in, ssm_lam_re, ssm_lam_im, ssm_log_dt, ssm_b_re, ssm_b_im, ssm_c_re, ssm_c_im, ssm_d, ssm_w_glu, ssm_b_glu, hgrn_lb_logits, hgrn_norm, diff_lq1, diff_lk1, diff_lq2, diff_lk2, diff_norm, rel_bias, gdn_conv, gdn_a_log, gdn_dt_bias, gdn_norm, w_out, g_norm2, w_up, w_down, g_final):
    w_in_tail = jnp.pad(w_in[:, :, N_IN_MAIN:], ((0, 0), (0, 0), (0, 128 - N_TAIL)))
    p = dict(w_ada=w_ada, b_ada=b_ada, g_norm1=g_norm1, w_in=w_in, w_in_tail=w_in_tail,
             ssm_lam_re=ssm_lam_re, ssm_lam_im=ssm_lam_im, ssm_log_dt=ssm_log_dt,
             ssm_b_re=ssm_b_re, ssm_b_im=ssm_b_im, ssm_c_re=ssm_c_re, ssm_c_im=ssm_c_im,
             ssm_d=ssm_d, ssm_w_glu=ssm_w_glu, ssm_b_glu=ssm_b_glu,
             hgrn_lb_logits=hgrn_lb_logits, hgrn_norm=hgrn_norm,
             diff_lq1=diff_lq1, diff_lk1=diff_lk1, diff_lq2=diff_lq2, diff_lk2=diff_lk2,
             diff_norm=diff_norm, rel_bias=rel_bias,
             gdn_conv=gdn_conv, gdn_a_log=gdn_a_log, gdn_dt_bias=gdn_dt_bias, gdn_norm=gdn_norm,
             w_out=w_out, g_norm2=g_norm2, w_up=w_up, w_down=w_down, g_final=g_final)

    nb_p, nb_s = c_prompt.shape[0], c_sample.shape[0]
    c_all = jnp.concatenate([c_prompt, c_sample], axis=0)
    rows = 16
    cs = jnp.pad(jax.nn.silu(c_all), ((0, rows - nb_p - nb_s), (0, 0))).astype(jnp.bfloat16)
    mods_p, mods_s = [], []
    for l in range(DEPTH):
        m = _matmul(cs, w_ada, l, 6 * D_MODEL, tm=rows, tn=2048, tk=1024, out_dtype=jnp.float32,
                    epilogue="bias", extras=(b_ada[l][None, :],),
                    extra_specs=(pl.BlockSpec((1, 2048), lambda i, j, k: (0, j)),))
        mods_p.append(m[:nb_p])
        mods_s.append(m[nb_p:nb_p + nb_s])

    y_prompt, sp = _trunk(x_prompt, mods_p, p, None)

    past = dict(cache_k=cache_k, cache_v=cache_v, page_table=page_table,
                ssm_re=state_ssm_re, ssm_im=state_ssm_im, hgrn=state_hgrn,
                gdn=state_gdn, conv=state_gdn_conv)
    y_sample, ss = _trunk(x_sample, mods_s, p, past)
    return (y_prompt, y_sample,
            sp[0], sp[1], sp[2], sp[3], sp[4], sp[5], sp[6],
            ss[0], ss[1], ss[2], ss[3], ss[4], ss[5], ss[6])
```
